```python
import math
import jax, jax.numpy as jnp
from jax import lax
import numpy as np

D_MODEL = 4096
BATCH = 2
SEQ = 8192
DEPTH = 4

N_EVEN = (DEPTH + 1) // 2
N_ODD = DEPTH // 2
EPS = 1e-6
CONV_CH = D_MODEL // 2
CONV_WIDTH = 31
DA_HEAD_DIM = 128
DA_V_DIM = 2 * DA_HEAD_DIM
DA_HEADS = D_MODEL // (2 * DA_V_DIM)
DA_QK = DA_HEADS * 2 * DA_HEAD_DIM
DA_V = DA_HEADS * DA_V_DIM
EVEN_IN = 2 * CONV_CH + 2 * DA_QK + DA_V
EVEN_OUT = CONV_CH + DA_V
Q_BLOCK = 128
D_INNER = 2 * D_MODEL
SSM_HEAD_DIM = 64
SSM_HEADS = D_INNER // SSM_HEAD_DIM
SSM_GROUPS = 8
SSM_HPG = SSM_HEADS // SSM_GROUPS
SSM_STATE = 128
SSM_CONV = 4
SSM_CONV_DIM = D_INNER + 2 * SSM_GROUPS * SSM_STATE
ODD_IN = D_INNER + SSM_CONV_DIM + SSM_HEADS
CHUNK = 128
MEM_LEN = 256
X_HEADS = 4
X_HEAD_DIM = 128
X_DIM = X_HEADS * X_HEAD_DIM
D_FF = 11008
FFN_CONV = 3

kernel_name = 'hybrid_conformer_diffattn_ssd_trunk'


def rms_norm(x, g):
    xf = x.astype(jnp.float32)
    y = xf * lax.rsqrt(jnp.mean(xf * xf, axis=-1, keepdims=True) + EPS)
    return (y * g.astype(jnp.float32)).astype(x.dtype)


def layer_norm(x, g, b):
    xf = x.astype(jnp.float32)
    mu = jnp.mean(xf, axis=-1, keepdims=True)
    xc = xf - mu
    y = xc * lax.rsqrt(jnp.mean(xc * xc, axis=-1, keepdims=True) + EPS)
    return (y * g.astype(jnp.float32) + b.astype(jnp.float32)).astype(x.dtype)


def causal_dwconv(x, w, b):
    K, C = w.shape
    y = lax.conv_general_dilated(
        x, w[:, None, :].astype(x.dtype), window_strides=(1,), padding=[(K - 1, 0)],
        dimension_numbers=('NWC', 'WIO', 'NWC'), feature_group_count=C)
    return y + b.astype(x.dtype)


def alibi_slopes(n_heads):
    return jnp.exp2(-8.0 * jnp.arange(1, n_heads + 1, dtype=jnp.float32) / n_heads)


def diff_attention(q, k, v, lam):
    Bsz, S = q.shape[0], q.shape[1]
    n_blk = S // Q_BLOCK
    slopes = alibi_slopes(DA_HEADS)
    kpos = jnp.arange(S)
    scale = DA_HEAD_DIM ** -0.5

    def block(i):
        start = i * Q_BLOCK
        qb = lax.dynamic_slice_in_dim(q, start, Q_BLOCK, axis=1)
        s = jnp.einsum('bqhcd,bkhcd->bhcqk', qb, k).astype(jnp.float32) * scale
        dist = (start + jnp.arange(Q_BLOCK))[:, None] - kpos[None, :]
        bias = -slopes[:, None, None] * dist.astype(jnp.float32)[None]
        s = jnp.where((dist >= 0)[None, None, None], s + bias[None, :, None], -jnp.inf)
        p = jax.nn.softmax(s, axis=-1)
        a = p[:, :, 0] - lam * p[:, :, 1]
        return jnp.einsum('bhqk,bkhe->bqhe', a.astype(v.dtype), v)

    out = lax.map(block, jnp.arange(n_blk))
    return jnp.moveaxis(out, 0, 1).reshape(Bsz, S, DA_HEADS, DA_V_DIM)


def even_mixer(xn, in_w, out_w, cv_w, cv_b, ln_g, ln_b, q_norm, k_norm, lq1, lk1, lq2, lk2, subln, lambda_init):
    Bsz, S, _ = xn.shape
    u = xn @ in_w
    c_a, c_g, q, k, v = jnp.split(
        u, [CONV_CH, 2 * CONV_CH, 2 * CONV_CH + DA_QK, 2 * CONV_CH + 2 * DA_QK], axis=-1)
    c = c_a * jax.nn.sigmoid(c_g)
    c = causal_dwconv(c, cv_w, cv_b)
    c = jax.nn.silu(layer_norm(c, ln_g, ln_b))
    q = rms_norm(q.reshape(Bsz, S, DA_HEADS, 2, DA_HEAD_DIM), q_norm)
    k = rms_norm(k.reshape(Bsz, S, DA_HEADS, 2, DA_HEAD_DIM), k_norm)
    v = v.reshape(Bsz, S, DA_HEADS, DA_V_DIM)
    lam = (jnp.exp(jnp.sum(lq1.astype(jnp.float32) * lk1.astype(jnp.float32)))
           - jnp.exp(jnp.sum(lq2.astype(jnp.float32) * lk2.astype(jnp.float32))) + lambda_init)
    o = diff_attention(q, k, v, lam)
    o = rms_norm(o, subln) * (1.0 - lambda_init)
    y = jnp.concatenate([c, o.reshape(Bsz, S, DA_V)], axis=-1)
    return y @ out_w


def ssd_scan(x, dt, A, Bm, Cm):
    Bsz, S, H, P = x.shape
    nc = S // CHUNK

    def chunk(t):
        return jnp.moveaxis(t.reshape((Bsz, nc, CHUNK) + t.shape[2:]), 1, 0)

    x_ = chunk(x.reshape(Bsz, S, SSM_GROUPS, SSM_HPG, P))
    dt_ = chunk(dt.reshape(Bsz, S, SSM_GROUPS, SSM_HPG))
    a_ = dt_ * A.reshape(SSM_GROUPS, SSM_HPG)
    B_ = chunk(Bm)
    C_ = chunk(Cm)
    tri = jnp.tril(jnp.ones((CHUNK, CHUNK), dtype=bool))[None, :, :, None, None]

    def step(state, inp):
        xc, dtc, ac, bc, cc = inp
        acum = jnp.cumsum(ac, axis=1)
        seg = acum[:, :, None] - acum[:, None]
        L = jnp.exp(jnp.where(tri, seg, -jnp.inf))
        cb = jnp.einsum('btgn,bsgn->btsg', cc, bc)
        xdt = xc * dtc[..., None]
        y = jnp.einsum('btsgj,bsgjp->btgjp', cb[..., None] * L, xdt)
        y = y + jnp.einsum('btgn,bgjpn->btgjp', cc, state) * jnp.exp(acum)[..., None]
        decay = jnp.exp(acum[:, -1:] - acum)
        state = (state * jnp.exp(acum[:, -1])[..., None, None]
                 + jnp.einsum('bsgn,bsgjp->bgjpn', bc, xdt * decay[..., None]))
        return state, y

    state0 = jnp.zeros((Bsz, SSM_GROUPS, SSM_HPG, P, SSM_STATE), jnp.float32)
    _, ys = lax.scan(step, state0, (x_, dt_, a_, B_, C_))
    return jnp.moveaxis(ys, 0, 1).reshape(Bsz, S, H, P)


def ssd_mixer(xn, in_w, conv_w, conv_b, dt_bias, A_log, D_skip, norm_w, out_w):
    Bsz, S, _ = xn.shape
    u = xn @ in_w
    z, xbc, dt = jnp.split(u, [D_INNER, D_INNER + SSM_CONV_DIM], axis=-1)
    xbc = jax.nn.silu(causal_dwconv(xbc, conv_w, conv_b))
    xs, Bm, Cm = jnp.split(xbc, [D_INNER, D_INNER + SSM_GROUPS * SSM_STATE], axis=-1)
    xs = xs.astype(jnp.float32).reshape(Bsz, S, SSM_HEADS, SSM_HEAD_DIM)
    Bm = Bm.astype(jnp.float32).reshape(Bsz, S, SSM_GROUPS, SSM_STATE)
    Cm = Cm.astype(jnp.float32).reshape(Bsz, S, SSM_GROUPS, SSM_STATE)
    dt = jax.nn.softplus(dt.astype(jnp.float32) + dt_bias.astype(jnp.float32))
    A = -jnp.exp(A_log.astype(jnp.float32))
    y = ssd_scan(xs, dt, A, Bm, Cm) + D_skip.astype(jnp.float32)[:, None] * xs
    g = y.reshape(Bsz, S, D_INNER) * jax.nn.silu(z.astype(jnp.float32))
    g = g.reshape(Bsz, S, SSM_GROUPS, D_INNER // SSM_GROUPS)
    g = g * lax.rsqrt(jnp.mean(g * g, axis=-1, keepdims=True) + EPS)
    g = g.reshape(Bsz, S, D_INNER) * norm_w.astype(jnp.float32)
    return g.astype(xn.dtype) @ out_w


def cross_attn(hn, memn, wq, wk, wv, wo, q_norm, k_norm):
    Bsz, S, _ = hn.shape
    M = memn.shape[1]
    q = rms_norm((hn @ wq).reshape(Bsz, S, X_HEADS, X_HEAD_DIM), q_norm)
    k = rms_norm((memn @ wk).reshape(Bsz, M, X_HEADS, X_HEAD_DIM), k_norm)
    v = (memn @ wv).reshape(Bsz, M, X_HEADS, X_HEAD_DIM)
    s = jnp.einsum('bshd,bmhd->bhsm', q, k).astype(jnp.float32) * (X_HEAD_DIM ** -0.5)
    p = jax.nn.softmax(s, axis=-1)
    o = jnp.einsum('bhsm,bmhd->bshd', p.astype(v.dtype), v).reshape(Bsz, S, X_DIM)
    return o @ wo


def conv_ffn(hn, w_in, cw, cb, w_out):
    a, b = jnp.split(hn @ w_in, 2, axis=-1)
    a = causal_dwconv(a, cw, cb)
    return (jax.nn.silu(a) * b) @ w_out


def setup_inputs(seed: int = 0) -> dict:
    key = jax.random.key(seed)
    ks = iter(jax.random.split(key, 64))
    f32 = jnp.float32

    def nrm(shape, fan_in):
        return jax.random.normal(next(ks), shape, f32) * (fan_in ** -0.5)

    def gain(shape):
        return 1.0 + 0.02 * jax.random.normal(next(ks), shape, f32)

    def small(shape, s=0.02):
        return s * jax.random.normal(next(ks), shape, f32)

    x = jax.random.normal(next(ks), (BATCH, SEQ, D_MODEL), f32)
    mem = jax.random.normal(next(ks), (BATCH, MEM_LEN, D_MODEL), f32)
    u_dt = jax.random.uniform(next(ks), (N_ODD, SSM_HEADS), f32)
    dt0 = jnp.exp(u_dt * (math.log(0.1) - math.log(0.001)) + math.log(0.001))
    dt_bias = dt0 + jnp.log(-jnp.expm1(-dt0))
    A_log = jnp.log(jax.random.uniform(next(ks), (N_ODD, SSM_HEADS), f32, 1.0, 16.0))
    return {
        'x': x,
        'mem': mem,
        'norm_mix': gain((DEPTH, D_MODEL)),
        'norm_cross': gain((DEPTH, D_MODEL)),
        'norm_mem': gain((DEPTH, D_MODEL)),
        'norm_ffn': gain((DEPTH, D_MODEL)),
        'xq_w': nrm((DEPTH, D_MODEL, X_DIM), D_MODEL),
        'xk_w': nrm((DEPTH, D_MODEL, X_DIM), D_MODEL),
        'xv_w': nrm((DEPTH, D_MODEL, X_DIM), D_MODEL),
        'xo_w': nrm((DEPTH, X_DIM, D_MODEL), X_DIM),
        'xq_norm': gain((DEPTH, X_HEAD_DIM)),
        'xk_norm': gain((DEPTH, X_HEAD_DIM)),
        'ffn_in_w': nrm((DEPTH, D_MODEL, 2 * D_FF), D_MODEL),
        'ffn_conv_w': nrm((DEPTH, FFN_CONV, D_FF), FFN_CONV),
        'ffn_conv_b': small((DEPTH, D_FF)),
        'ffn_out_w': nrm((DEPTH, D_FF, D_MODEL), D_FF),
        'ev_in_w': nrm((N_EVEN, D_MODEL, EVEN_IN), D_MODEL),
        'ev_out_w': nrm((N_EVEN, EVEN_OUT, D_MODEL), EVEN_OUT),
        'cv_conv_w': nrm((N_EVEN, CONV_WIDTH, CONV_CH), CONV_WIDTH),
        'cv_conv_b': small((N_EVEN, CONV_CH)),
        'cv_ln_g': gain((N_EVEN, CONV_CH)),
        'cv_ln_b': small((N_EVEN, CONV_CH)),
        'da_q_norm': gain((N_EVEN, DA_HEAD_DIM)),
        'da_k_norm': gain((N_EVEN, DA_HEAD_DIM)),
        'da_lq1': small((N_EVEN, DA_HEAD_DIM), 0.1),
        'da_lk1': small((N_EVEN, DA_HEAD_DIM), 0.1),
        'da_lq2': small((N_EVEN, DA_HEAD_DIM), 0.1),
        'da_lk2': small((N_EVEN, DA_HEAD_DIM), 0.1),
        'da_subln': gain((N_EVEN, DA_V_DIM)),
        'm_in_w': nrm((N_ODD, D_MODEL, ODD_IN), D_MODEL),
        'm_conv_w': nrm((N_ODD, SSM_CONV, SSM_CONV_DIM), SSM_CONV),
        'm_conv_b': small((N_ODD, SSM_CONV_DIM)),
        'm_dt_bias': dt_bias,
        'm_A_log': A_log,
        'm_D': gain((N_ODD, SSM_HEADS)),
        'm_norm': gain((N_ODD, D_INNER)),
        'm_out_w': nrm((N_ODD, D_INNER, D_MODEL), D_INNER),
    }


def reference(x, mem, norm_mix, norm_cross, norm_mem, norm_ffn, xq_w, xk_w, xv_w, xo_w, xq_norm, xk_norm,
              ffn_in_w, ffn_conv_w, ffn_conv_b, ffn_out_w, ev_in_w, ev_out_w, cv_conv_w, cv_conv_b,
              cv_ln_g, cv_ln_b, da_q_norm, da_k_norm, da_lq1, da_lk1, da_lq2, da_lk2, da_subln,
              m_in_w, m_conv_w, m_conv_b, m_dt_bias, m_A_log, m_D, m_norm, m_out_w):
    h = x
    for i in range(DEPTH):
        hn = rms_norm(h, norm_mix[i])
        if i % 2 == 0:
            e = i // 2
            lambda_init = 0.8 - 0.6 * math.exp(-0.3 * i)
            h = h + even_mixer(hn, ev_in_w[e], ev_out_w[e], cv_conv_w[e], cv_conv_b[e], cv_ln_g[e],
                               cv_ln_b[e], da_q_norm[e], da_k_norm[e], da_lq1[e], da_lk1[e],
                               da_lq2[e], da_lk2[e], da_subln[e], lambda_init)
        else:
            o = i // 2
            h = h + ssd_mixer(hn, m_in_w[o], m_conv_w[o], m_conv_b[o], m_dt_bias[o], m_A_log[o],
                              m_D[o], m_norm[o], m_out_w[o])
        h = h + cross_attn(rms_norm(h, norm_cross[i]), rms_norm(mem, norm_mem[i]), xq_w[i], xk_w[i],
                           xv_w[i], xo_w[i], xq_norm[i], xk_norm[i])
        h = h + conv_ffn(rms_norm(h, norm_ffn[i]), ffn_in_w[i], ffn_conv_w[i], ffn_conv_b[i], ffn_out_w[i])
    return h
```

```python
import functools
import math

import jax
import jax.numpy as jnp
from jax import lax
from jax.experimental import pallas as pl
from jax.experimental.pallas import tpu as pltpu

F32 = jnp.float32
BF16 = jnp.bfloat16
EPS = 1e-6

V7X_VMEM_LIMIT_BYTES = 56 * 1024 * 1024
LANE = 128

SSM_STATE = 128
SSM_CHUNK = 128
FFN_PAD = 1024

MM_BM = 1024
MM_BN = 1024
MM_BK = 4096
NORM_ROWS = 256
CONF_ROWS = 128
CONF_HALO = 32
ATT_BQ = 512
CROSS_ROWS = 256
FFN_BM = 1024
FFN_BN = 512
SCONV_ROWS = 256
SCONV_HALO = 16


def _tile(dim, pref, align=LANE):
    if dim <= pref:
        return dim
    t = (pref // align) * align
    while t >= align:
        if dim % t == 0:
            return t
        t -= align
    return dim


def _params(*sem):
    return pltpu.CompilerParams(dimension_semantics=sem, vmem_limit_bytes=V7X_VMEM_LIMIT_BYTES)


def _silu(x):
    return x * jax.nn.sigmoid(x)


def _rmsnorm_kernel(x_ref, g_ref, o_ref):
    x = x_ref[...].astype(F32)
    inv = lax.rsqrt(jnp.mean(x * x, axis=-1, keepdims=True) + EPS)
    o_ref[...] = (x * inv * g_ref[...]).astype(o_ref.dtype)


def rmsnorm(x, g, out_dtype=BF16):
    T, D = x.shape
    br = _tile(T, NORM_ROWS, 8)
    return pl.pallas_call(
        _rmsnorm_kernel,
        grid=(T // br,),
        in_specs=[pl.BlockSpec((br, D), lambda i: (i, 0)), pl.BlockSpec((1, D), lambda i: (0, 0))],
        out_specs=pl.BlockSpec((br, D), lambda i: (i, 0)),
        out_shape=jax.ShapeDtypeStruct((T, D), out_dtype),
        compiler_params=_params("parallel"),
        name="rmsnorm",
    )(x, g.reshape(1, D).astype(F32))


def _mm_kernel(*refs, nk, has_res):
    if has_res:
        a_ref, w_ref, r_ref, o_ref = refs[:4]
    else:
        a_ref, w_ref, o_ref = refs[:3]
        r_ref = None
    part = jnp.dot(a_ref[...], w_ref[...], preferred_element_type=F32)

    def finish(total):
        if has_res:
            total = total + r_ref[...]
        o_ref[...] = total.astype(o_ref.dtype)

    if nk == 1:
        finish(part)
        return
    acc_ref = refs[-1]
    k = pl.program_id(2)

    @pl.when(k == 0)
    def _():
        acc_ref[...] = part

    @pl.when(jnp.logical_and(k > 0, k < nk - 1))
    def _():
        acc_ref[...] += part

    @pl.when(k == nk - 1)
    def _():
        finish(acc_ref[...] + part)


def matmul(a, w, res=None, out_dtype=BF16, bm=MM_BM, bn=MM_BN, bk=MM_BK):
    M, K = a.shape
    N = w.shape[1]
    bm, bn, bk = _tile(M, bm, 8), _tile(N, bn), _tile(K, bk)
    nk = K // bk
    in_specs = [pl.BlockSpec((bm, bk), lambda i, j, k: (i, k)), pl.BlockSpec((bk, bn), lambda i, j, k: (k, j))]
    args = [a, w]
    if res is not None:
        in_specs.append(pl.BlockSpec((bm, bn), lambda i, j, k: (i, j)))
        args.append(res)
    return pl.pallas_call(
        functools.partial(_mm_kernel, nk=nk, has_res=res is not None),
        grid=(M // bm, N // bn, nk),
        in_specs=in_specs,
        out_specs=pl.BlockSpec((bm, bn), lambda i, j, k: (i, j)),
        out_shape=jax.ShapeDtypeStruct((M, N), out_dtype),
        scratch_shapes=[pltpu.VMEM((bm, bn), F32)] if nk > 1 else [],
        compiler_params=_params("parallel", "parallel", "arbitrary"),
        name="matmul",
    )(*args)


def _headnorm_kernel(x_ref, g_ref, o_ref, *, dh):
    width = x_ref.shape[-1]
    g = g_ref[0]
    for c0 in range(0, width, dh):
        x = x_ref[:, c0:c0 + dh].astype(F32)
        inv = lax.rsqrt(jnp.mean(x * x, axis=-1, keepdims=True) + EPS)
        o_ref[:, c0:c0 + dh] = (x * inv * g).astype(o_ref.dtype)


def headnorm(x, gains, col_block0, width):
    T = x.shape[0]
    n, dh = gains.shape
    br = _tile(T, NORM_ROWS, 8)
    return pl.pallas_call(
        functools.partial(_headnorm_kernel, dh=dh),
        grid=(T // br, n),
        in_specs=[
            pl.BlockSpec((br, width), lambda i, j: (i, col_block0 + j)),
            pl.BlockSpec((1, 1, dh), lambda i, j: (j, 0, 0)),
        ],
        out_specs=pl.BlockSpec((br, width), lambda i, j: (i, j)),
        out_shape=jax.ShapeDtypeStruct((T, n * width), BF16),
        compiler_params=_params("parallel", "parallel"),
        name="headnorm",
    )(x, gains.reshape(n, 1, dh).astype(F32))


def _conformer_kernel(a_ref, g_ref, ha_ref, hg_ref, w_ref, b_ref, lg_ref, lb_ref, o_ref, buf_ref, y_ref,
                      *, blocks_per_seq):
    bs, C = a_ref.shape
    halo = ha_ref.shape[0]
    kw = w_ref.shape[0]
    first = (pl.program_id(0) % blocks_per_seq) == 0
    buf_ref[halo:halo + bs, :] = a_ref[...].astype(F32) * jax.nn.sigmoid(g_ref[...].astype(F32))
    hc = ha_ref[...].astype(F32) * jax.nn.sigmoid(hg_ref[...].astype(F32))
    buf_ref[0:halo, :] = jnp.where(first, 0.0, hc)
    off = halo - (kw - 1)
    rs, cs = min(bs, 64), min(C, 512)
    for r0 in range(0, bs, rs):
        for c0 in range(0, C, cs):
            acc = jnp.broadcast_to(b_ref[:, c0:c0 + cs], (rs, cs))
            for k in range(kw):
                acc = acc + w_ref[k:k + 1, c0:c0 + cs] * buf_ref[off + k + r0:off + k + r0 + rs, c0:c0 + cs]
            y_ref[r0:r0 + rs, c0:c0 + cs] = acc
    y = y_ref[...]
    yc = y - jnp.mean(y, axis=-1, keepdims=True)
    z = yc * lax.rsqrt(jnp.mean(yc * yc, axis=-1, keepdims=True) + EPS) * lg_ref[...] + lb_ref[...]
    o_ref[...] = _silu(z).astype(o_ref.dtype)


def conformer_conv(u, conv_w, conv_b, ln_g, ln_b, seq):
    T = u.shape[0]
    kw, C = conv_w.shape
    bs = _tile(seq, CONF_ROWS, CONF_HALO)
    halo = CONF_HALO
    assert kw - 1 <= halo and bs % halo == 0
    hb = bs // halo
    row = lambda v: v.reshape(1, C).astype(F32)
    return pl.pallas_call(
        functools.partial(_conformer_kernel, blocks_per_seq=seq // bs),
        grid=(T // bs,),
        in_specs=[
            pl.BlockSpec((bs, C), lambda i: (i, 0)),
            pl.BlockSpec((bs, C), lambda i: (i, 1)),
            pl.BlockSpec((halo, C), lambda i: (jnp.maximum(i * hb - 1, 0), 0)),
            pl.BlockSpec((halo, C), lambda i: (jnp.maximum(i * hb - 1, 0), 1)),
            pl.BlockSpec((kw, C), lambda i: (0, 0)),
            pl.BlockSpec((1, C), lambda i: (0, 0)),
            pl.BlockSpec((1, C), lambda i: (0, 0)),
            pl.BlockSpec((1, C), lambda i: (0, 0)),
        ],
        out_specs=pl.BlockSpec((bs, C), lambda i: (i, 0)),
        out_shape=jax.ShapeDtypeStruct((T, C), BF16),
        scratch_shapes=[pltpu.VMEM((bs + halo, C), F32), pltpu.VMEM((bs, C), F32)],
        compiler_params=_params("parallel"),
        name="conformer_conv",
    )(u, u, u, u, conv_w.astype(F32), row(conv_b), row(ln_g), row(ln_b))


def _diff_attn_kernel(lam_ref, slope_ref, q_ref, k_ref, v_ref, sub_ref, o_ref, acc_ref, *, out_scale):
    bq, two_dh = q_ref.shape
    dh = two_dh // 2
    bk = bq
    qi = pl.program_id(2)
    slope = slope_ref[0, pl.program_id(1)]
    rel = (lax.broadcasted_iota(jnp.int32, (bq, bk), 1) - lax.broadcasted_iota(jnp.int32, (bq, bk), 0)).astype(F32)
    comps = []
    for c in range(2):
        qc = q_ref[:, c * dh:(c + 1) * dh]

        def step(j, carry, masked, qc=qc, c=c):
            m, l = carry
            kc = k_ref[pl.ds(pl.multiple_of(j * bk, bk), bk), c * dh:(c + 1) * dh]
            s = lax.dot_general(qc, kc, (((1,), (1,)), ((), ())), preferred_element_type=F32)
            s = s + slope * (rel + ((j - qi) * bk).astype(F32))
            if masked:
                s = jnp.where(rel <= 0.0, s, -jnp.inf)
            m_new = jnp.maximum(m, jnp.max(s, axis=-1, keepdims=True))
            alpha = jnp.exp(m - m_new)
            p = jnp.exp(s - m_new)
            l = alpha * l + jnp.sum(p, axis=-1, keepdims=True)
            vb = v_ref[pl.ds(pl.multiple_of(j * bk, bk), bk), :]
            acc_ref[...] = alpha * acc_ref[...] + jnp.dot(p.astype(BF16), vb, preferred_element_type=F32)
            return m_new, l

        acc_ref[...] = jnp.zeros_like(acc_ref)
        init = (jnp.full((bq, 1), -jnp.inf, F32), jnp.zeros((bq, 1), F32))
        carry = lax.fori_loop(0, qi, functools.partial(step, masked=False), init)
        _, l = step(qi, carry, masked=True)
        comps.append(acc_ref[...] / l)
    o = comps[0] - lam_ref[0, 0] * comps[1]
    o = o * lax.rsqrt(jnp.mean(o * o, axis=-1, keepdims=True) + EPS) * sub_ref[...] * out_scale
    o_ref[...] = o.astype(o_ref.dtype)


def diff_attention(qk, u, v_col_block0, lam, subln, lambda_init, batch, seq, n_heads):
    T = qk.shape[0]
    dv = subln.shape[-1]
    bq = _tile(seq, ATT_BQ, 8)
    nq = seq // bq
    slopes = jnp.exp2(-8.0 * jnp.arange(1, n_heads + 1, dtype=F32) / n_heads).reshape(1, n_heads)
    return pl.pallas_call(
        functools.partial(_diff_attn_kernel, out_scale=1.0 - lambda_init),
        grid=(batch, n_heads, nq),
        in_specs=[
            pl.BlockSpec(memory_space=pltpu.SMEM),
            pl.BlockSpec(memory_space=pltpu.SMEM),
            pl.BlockSpec((bq, dv), lambda b, h, i: (b * nq + i, h)),
            pl.BlockSpec((seq, dv), lambda b, h, i: (b, n_heads + h)),
            pl.BlockSpec((seq, dv), lambda b, h, i: (b, v_col_block0 + h)),
            pl.BlockSpec((1, dv), lambda b, h, i: (0, 0)),
        ],
        out_specs=pl.BlockSpec((bq, dv), lambda b, h, i: (b * nq + i, h)),
        out_shape=jax.ShapeDtypeStruct((T, n_heads * dv), BF16),
        scratch_shapes=[pltpu.VMEM((bq, dv), F32)],
        compiler_params=_params("parallel", "parallel", "arbitrary"),
        name="diff_attention",
    )(lam.reshape(1, 1).astype(F32), slopes, qk, qk, u, subln.reshape(1, dv).astype(F32))


def _sconv_kernel(x_ref, hx_ref, w_ref, b_ref, o_ref, buf_ref, *, blocks_per_seq):
    bs, _ = x_ref.shape
    halo = hx_ref.shape[0]
    kw = w_ref.shape[0]
    first = (pl.program_id(1) % blocks_per_seq) == 0
    buf_ref[halo:halo + bs, :] = x_ref[...].astype(F32)
    buf_ref[0:halo, :] = jnp.where(first, 0.0, hx_ref[...].astype(F32))
    off = halo - (kw - 1)
    acc = b_ref[...] + w_ref[0:1, :] * buf_ref[off:off + bs, :]
    for k in range(1, kw):
        acc = acc + w_ref[k:k + 1, :] * buf_ref[off + k:off + k + bs, :]
    o_ref[...] = _silu(acc).astype(o_ref.dtype)


def ssd_conv(xbc, conv_w, conv_b, seq):
    T, C = xbc.shape
    kw = conv_w.shape[0]
    bs = _tile(seq, SCONV_ROWS, SCONV_HALO)
    halo = SCONV_HALO
    bc = _tile(C, 1024)
    hb = bs // halo
    return pl.pallas_call(
        functools.partial(_sconv_kernel, blocks_per_seq=seq // bs),
        grid=(C // bc, T // bs),
        in_specs=[
            pl.BlockSpec((bs, bc), lambda j, i: (i, j)),
            pl.BlockSpec((halo, bc), lambda j, i: (jnp.maximum(i * hb - 1, 0), j)),
            pl.BlockSpec((kw, bc), lambda j, i: (0, j)),
            pl.BlockSpec((1, bc), lambda j, i: (0, j)),
        ],
        out_specs=pl.BlockSpec((bs, bc), lambda j, i: (i, j)),
        out_shape=jax.ShapeDtypeStruct((T, C), BF16),
        scratch_shapes=[pltpu.VMEM((bs + halo, bc), F32)],
        compiler_params=_params("parallel", "parallel"),
        name="ssd_conv",
    )(xbc, xbc, conv_w.astype(F32), conv_b.reshape(1, C).astype(F32))


def _split3(x):
    hi = x.astype(BF16)
    r = x - hi.astype(F32)
    mid = r.astype(BF16)
    lo = (r - mid.astype(F32)).astype(BF16)
    return hi, mid, lo


def _softplus(x):
    return jnp.maximum(x, 0.0) + jnp.log1p(jnp.exp(-jnp.abs(x)))


def _ssd_kernel(x_ref, z_ref, b_ref, c_ref, dt_ref, dtt_ref, a_ref, at_ref, db_ref, dbt_ref, dsk_ref, nw_ref,
                o_ref, state_ref, y_ref, *, hpg, hd):
    L = x_ref.shape[0]

    @pl.when(pl.program_id(2) == 0)
    def _():
        state_ref[...] = jnp.zeros_like(state_ref)

    dt = _softplus(dt_ref[0, 0] + db_ref[0])
    dtt = _softplus(dtt_ref[0, 0] + dbt_ref[0])
    a = dt * a_ref[0]
    at = dtt * at_ref[0]
    ri = lax.broadcasted_iota(jnp.int32, (L, L), 0)
    ci = lax.broadcasted_iota(jnp.int32, (L, L), 1)
    causal = ci <= ri
    tril = causal.astype(BF16)
    triu = (ri <= ci).astype(BF16)
    acum = sum(jnp.dot(tril, p, preferred_element_type=F32) for p in _split3(a))
    acumt = sum(jnp.dot(p, triu, preferred_element_type=F32) for p in _split3(at))
    a_last = acum[L - 1:L, :]
    ea = jnp.exp(acum)
    wst = jnp.exp(a_last - acum) * dt
    ea_last = jnp.exp(a_last)

    bm = b_ref[...]
    cm = c_ref[...]
    cb = lax.dot_general(cm, bm, (((1,), (1,)), ((), ())), preferred_element_type=F32)
    bm32 = bm.astype(F32)
    for j in range(hpg):
        xj = x_ref[:, j * hd:(j + 1) * hd]
        seg = acum[:, j:j + 1] - acumt[j:j + 1, :]
        mj = cb * jnp.where(causal, jnp.exp(seg), 0.0) * dtt[j:j + 1, :]
        st = state_ref[j]
        yj = jnp.dot(mj.astype(BF16), xj, preferred_element_type=F32)
        yj = yj + jnp.dot(cm, st.astype(BF16), preferred_element_type=F32) * ea[:, j:j + 1]
        yj = yj + dsk_ref[:, j * hd:(j + 1) * hd] * xj.astype(F32)
        y_ref[:, j * hd:(j + 1) * hd] = yj
        bw = (bm32 * wst[:, j:j + 1]).astype(BF16)
        upd = lax.dot_general(bw, xj, (((0,), (0,)), ((), ())), preferred_element_type=F32)
        state_ref[j] = st * ea_last[:, j:j + 1] + upd

    g = y_ref[...] * _silu(z_ref[...].astype(F32))
    g = g * lax.rsqrt(jnp.mean(g * g, axis=-1, keepdims=True) + EPS) * nw_ref[...]
    o_ref[...] = g.astype(o_ref.dtype)


def ssd_scan(xbc, z, dt_raw, dt_bias, a_log, d_skip, norm_w, batch, seq, d_inner):
    T = xbc.shape[0]
    H = dt_raw.shape[-1]
    N = SSM_STATE
    G = (xbc.shape[1] - d_inner) // (2 * N)
    hpg = H // G
    hd = d_inner // H
    gw = hpg * hd
    L = _tile(seq, SSM_CHUNK, 8)
    nc = seq // L
    assert gw % LANE == 0 and gw % N == 0
    b0 = d_inner // N
    dtg = dt_raw.reshape(batch, seq, G, hpg).transpose(0, 2, 1, 3)
    dttg = dtg.transpose(0, 1, 3, 2)
    a = -jnp.exp(a_log.astype(F32)).reshape(G, 1, hpg)
    db = dt_bias.astype(F32).reshape(G, 1, hpg)
    dsk = jnp.repeat(d_skip.astype(F32), hd).reshape(1, d_inner)
    row = lambda b, g, c: (b * nc + c, g)
    return pl.pallas_call(
        functools.partial(_ssd_kernel, hpg=hpg, hd=hd),
        grid=(batch, G, nc),
        in_specs=[
            pl.BlockSpec((L, gw), row),
            pl.BlockSpec((L, gw), row),
            pl.BlockSpec((L, N), lambda b, g, c: (b * nc + c, b0 + g)),
            pl.BlockSpec((L, N), lambda b, g, c: (b * nc + c, b0 + G + g)),
            pl.BlockSpec((1, 1, L, hpg), lambda b, g, c: (b, g, c, 0)),
            pl.BlockSpec((1, 1, hpg, L), lambda b, g, c: (b, g, 0, c)),
            pl.BlockSpec((1, 1, hpg), lambda b, g, c: (g, 0, 0)),
            pl.BlockSpec((1, hpg, 1), lambda b, g, c: (g, 0, 0)),
            pl.BlockSpec((1, 1, hpg), lambda b, g, c: (g, 0, 0)),
            pl.BlockSpec((1, hpg, 1), lambda b, g, c: (g, 0, 0)),
            pl.BlockSpec((1, gw), lambda b, g, c: (0, g)),
            pl.BlockSpec((1, gw), lambda b, g, c: (0, g)),
        ],
        out_specs=pl.BlockSpec((L, gw), row),
        out_shape=jax.ShapeDtypeStruct((T, d_inner), BF16),
        scratch_shapes=[pltpu.VMEM((hpg, N, hd), F32), pltpu.VMEM((L, gw), F32)],
        compiler_params=_params("parallel", "parallel", "arbitrary"),
        name="ssd_scan",
    )(xbc, z, xbc, xbc, dtg, dttg, a, a.reshape(G, hpg, 1), db, db.reshape(G, hpg, 1), dsk,
      norm_w.reshape(1, d_inner).astype(F32))


def _cross_kernel(h_ref, gx_ref, wq_ref, qn_ref, k_ref, v_ref, wo_ref, gf_ref, ho_ref, xn_ref, *, dh):
    h = h_ref[...]
    hn = (h * lax.rsqrt(jnp.mean(h * h, axis=-1, keepdims=True) + EPS) * gx_ref[...]).astype(BF16)
    q = jnp.dot(hn, wq_ref[...], preferred_element_type=F32)
    xd = q.shape[-1]
    heads = []
    for c0 in range(0, xd, dh):
        qh = q[:, c0:c0 + dh]
        qh = qh * lax.rsqrt(jnp.mean(qh * qh, axis=-1, keepdims=True) + EPS) * qn_ref[...]
        s = lax.dot_general(qh.astype(BF16), k_ref[0, :, c0:c0 + dh], (((1,), (1,)), ((), ())),
                            preferred_element_type=F32)
        s = s - jnp.max(s, axis=-1, keepdims=True)
        p = jnp.exp(s)
        p = p / jnp.sum(p, axis=-1, keepdims=True)
        heads.append(jnp.dot(p.astype(BF16), v_ref[0, :, c0:c0 + dh], preferred_element_type=F32))
    o = jnp.concatenate(heads, axis=-1).astype(BF16)
    hnew = h + jnp.dot(o, wo_ref[...], preferred_element_type=F32)
    ho_ref[...] = hnew
    xn = hnew * lax.rsqrt(jnp.mean(hnew * hnew, axis=-1, keepdims=True) + EPS) * gf_ref[...]
    xn_ref[...] = xn.astype(xn_ref.dtype)


def cross_attention(h, g_cross, wq, q_norm, k, v, wo, g_ffn, seq):
    T, D = h.shape
    X = wq.shape[1]
    M = k.shape[1]
    dh = q_norm.shape[-1]
    bm = _tile(seq, CROSS_ROWS, 8)
    per_seq = seq // bm
    const = lambda i: (0, 0)
    resident = dict(pipeline_mode=pl.Buffered(1))
    return pl.pallas_call(
        functools.partial(_cross_kernel, dh=dh),
        grid=(T // bm,),
        in_specs=[
            pl.BlockSpec((bm, D), lambda i: (i, 0)),
            pl.BlockSpec((1, D), const),
            pl.BlockSpec((D, X), const, **resident),
            pl.BlockSpec((1, dh), const),
            pl.BlockSpec((1, M, X), lambda i: (i // per_seq, 0, 0)),
            pl.BlockSpec((1, M, X), lambda i: (i // per_seq, 0, 0)),
            pl.BlockSpec((X, D), const, **resident),
            pl.BlockSpec((1, D), const),
        ],
        out_specs=[pl.BlockSpec((bm, D), lambda i: (i, 0)), pl.BlockSpec((bm, D), lambda i: (i, 0))],
        out_shape=[jax.ShapeDtypeStruct((T, D), F32), jax.ShapeDtypeStruct((T, D), BF16)],
        compiler_params=_params("parallel"),
        name="cross_attention",
    )(h, g_cross.reshape(1, D).astype(F32), wq, q_norm.reshape(1, dh).astype(F32), k, v, wo,
      g_ffn.reshape(1, D).astype(F32))


def _ffn_in_kernel(x_ref, wa_ref, wb_ref, cw_ref, cb_ref, o_ref, buf_ref, *, tiles_per_seq):
    bm = x_ref.shape[0]
    kw = cw_ref.shape[0]
    first = (pl.program_id(1) % tiles_per_seq) == 0
    @pl.when(first)
    def _():
        buf_ref[0:8, :] = jnp.zeros((8, buf_ref.shape[1]), F32)

    @pl.when(jnp.logical_not(first))
    def _():
        buf_ref[0:8, :] = buf_ref[bm:bm + 8, :]

    x = x_ref[...]
    buf_ref[8:8 + bm, :] = jnp.dot(x, wa_ref[...], preferred_element_type=F32)
    b = jnp.dot(x, wb_ref[...], preferred_element_type=F32)
    off = 8 - (kw - 1)
    acc = cb_ref[...] + cw_ref[0:1, :] * buf_ref[off:off + bm, :]
    for k in range(1, kw):
        acc = acc + cw_ref[k:k + 1, :] * buf_ref[off + k:off + k + bm, :]
    o_ref[...] = (_silu(acc) * b).astype(o_ref.dtype)


def ffn_in(xn, wa, wb, conv_w, conv_b, seq):
    T, D = xn.shape
    F = wa.shape[1]
    bm = _tile(seq, FFN_BM, 8)
    bn = _tile(F, FFN_BN)
    return pl.pallas_call(
        functools.partial(_ffn_in_kernel, tiles_per_seq=seq // bm),
        grid=(F // bn, T // bm),
        in_specs=[
            pl.BlockSpec((bm, D), lambda j, i: (i, 0)),
            pl.BlockSpec((D, bn), lambda j, i: (0, j)),
            pl.BlockSpec((D, bn), lambda j, i: (0, j)),
            pl.BlockSpec((conv_w.shape[0], bn), lambda j, i: (0, j)),
            pl.BlockSpec((1, bn), lambda j, i: (0, j)),
        ],
        out_specs=pl.BlockSpec((bm, bn), lambda j, i: (i, j)),
        out_shape=jax.ShapeDtypeStruct((T, F), BF16),
        scratch_shapes=[pltpu.VMEM((bm + 8, bn), F32)],
        compiler_params=_params("parallel", "arbitrary"),
        name="ffn_in",
    )(xn, wa, wb, conv_w, conv_b)


def _pad_cols(w, n):
    return jnp.pad(w, ((0, 0), (0, n - w.shape[1])))


def kernel(x, mem, norm_mix, norm_cross, norm_mem, norm_ffn, xq_w, xk_w, xv_w, xo_w, xq_norm, xk_norm, ffn_in_w, ffn_conv_w, ffn_conv_b, ffn_out_w, ev_in_w, ev_out_w, cv_conv_w, cv_conv_b, cv_ln_g, cv_ln_b, da_q_norm, da_k_norm, da_lq1, da_lk1, da_lq2, da_lk2, da_subln, m_in_w, m_conv_w, m_conv_b, m_dt_bias, m_A_log, m_D, m_norm, m_out_w):
    batch, seq, D = x.shape
    depth = norm_mix.shape[0]
    T = batch * seq
    mem_len = mem.shape[1]
    x_dim = xq_w.shape[-1]
    x_dh = xq_norm.shape[-1]
    d_ff = ffn_conv_w.shape[-1]
    f_pad = -(-d_ff // FFN_PAD) * FFN_PAD
    conv_ch = cv_conv_w.shape[-1]
    da_dh = da_q_norm.shape[-1]
    da_dv = da_subln.shape[-1]
    da_qk = (ev_in_w.shape[-1] - 2 * conv_ch) // 3
    da_heads = da_qk // (2 * da_dh)
    d_inner = m_norm.shape[-1]
    ssm_heads = m_dt_bias.shape[-1]
    assert da_dv == 2 * da_dh and conv_ch % da_qk == 0 and da_qk % da_dv == 0

    h = x.reshape(T, D)
    mem2 = mem.reshape(batch * mem_len, D)

    for i in range(depth):
        xn = rmsnorm(h, norm_mix[i])
        if i % 2 == 0:
            e = i // 2
            lambda_init = 0.8 - 0.6 * math.exp(-0.3 * i)
            u = matmul(xn, ev_in_w[e].astype(BF16))
            c = conformer_conv(u, cv_conv_w[e], cv_conv_b[e], cv_ln_g[e], cv_ln_b[e], seq)
            gains = jnp.stack([da_q_norm[e].astype(F32) * (da_dh ** -0.5), da_k_norm[e].astype(F32)])
            qk = headnorm(u, gains, 2 * conv_ch // da_qk, da_qk)
            lam = (jnp.exp(jnp.sum(da_lq1[e].astype(F32) * da_lk1[e].astype(F32)))
                   - jnp.exp(jnp.sum(da_lq2[e].astype(F32) * da_lk2[e].astype(F32))) + lambda_init)
            o = diff_attention(qk, u, (2 * conv_ch + 2 * da_qk) // da_dv, lam, da_subln[e], lambda_init,
                               batch, seq, da_heads)
            y = jnp.concatenate([c, o], axis=-1)
            h = matmul(y, ev_out_w[e].astype(BF16), res=h, out_dtype=F32)
        else:
            o_ = i // 2
            w_in = m_in_w[o_]
            z = matmul(xn, w_in[:, :d_inner].astype(BF16))
            xbc = matmul(xn, w_in[:, d_inner:w_in.shape[1] - ssm_heads].astype(BF16))
            dt_raw = matmul(xn, w_in[:, w_in.shape[1] - ssm_heads:].astype(BF16), out_dtype=F32)
            xbc = ssd_conv(xbc, m_conv_w[o_], m_conv_b[o_], seq)
            g = ssd_scan(xbc, z, dt_raw, m_dt_bias[o_], m_A_log[o_], m_D[o_], m_norm[o_], batch, seq, d_inner)
            h = matmul(g, m_out_w[o_].astype(BF16), res=h, out_dtype=F32, bk=2048)

        memn = rmsnorm(mem2, norm_mem[i])
        kv = matmul(memn, jnp.concatenate([xk_w[i], xv_w[i]], axis=1).astype(BF16), out_dtype=F32)
        kx = headnorm(kv, xk_norm[i].reshape(1, x_dh), 0, x_dim).reshape(batch, mem_len, x_dim)
        vx = kv[:, x_dim:].astype(BF16).reshape(batch, mem_len, x_dim)
        h, xn = cross_attention(h, norm_cross[i], xq_w[i].astype(BF16), xq_norm[i].astype(F32) * (x_dh ** -0.5),
                                kx, vx, xo_w[i].astype(BF16), norm_ffn[i], seq)

        wa = _pad_cols(ffn_in_w[i][:, :d_ff], f_pad).astype(BF16)
        wb = _pad_cols(ffn_in_w[i][:, d_ff:], f_pad).astype(BF16)
        cw = _pad_cols(ffn_conv_w[i].astype(F32), f_pad)
        cb = _pad_cols(ffn_conv_b[i].astype(F32).reshape(1, d_ff), f_pad)
        act = ffn_in(xn, wa, wb, cw, cb, seq)
        w_out = jnp.pad(ffn_out_w[i], ((0, f_pad - d_ff), (0, 0))).astype(BF16)
        h = matmul(act, w_out, res=h, out_dtype=F32, bk=2816)
    return h.reshape(batch, seq, D)
```

```python
import functools
import math

import jax
import jax.numpy as jnp
from jax import lax
from jax.experimental import pallas as pl
from jax.experimental.pallas import tpu as pltpu

F32 = jnp.float32
BF16 = jnp.bfloat16
EPS = 1e-6

V7X_VMEM_LIMIT_BYTES = 56 * 1024 * 1024
LANE = 128

SSM_STATE = 128
SSM_CHUNK = 128
SSM_BLOCK_CHUNKS = 4
FFN_PAD = 1024

MM_BM = 1024
MM_BN = 1024
MM_BK = 4096
MM_WIDE_BN = 2048
MM_WIDE_BK = 1408
NORM_ROWS = 256
CONF_ROWS = 128
CONF_HALO = 32
ATT_BQ = 512
ATT_MAX_SHIFT = 48.0
CROSS_ROWS = 256
FFN_BM = 1024
FFN_BN = 512
SCONV_ROWS = 256
SCONV_HALO = 16


def _tile(dim, pref, align=LANE):
    if dim <= pref:
        return dim
    t = (pref // align) * align
    while t >= align:
        if dim % t == 0:
            return t
        t -= align
    return dim


def _params(*sem):
    return pltpu.CompilerParams(dimension_semantics=sem, vmem_limit_bytes=V7X_VMEM_LIMIT_BYTES)


def _silu(x):
    return x * jax.nn.sigmoid(x)


def _rmsnorm_kernel(x_ref, g_ref, o_ref):
    x = x_ref[...].astype(F32)
    inv = lax.rsqrt(jnp.mean(x * x, axis=-1, keepdims=True) + EPS)
    o_ref[...] = (x * inv * g_ref[...]).astype(o_ref.dtype)


def rmsnorm(x, g, out_dtype=BF16):
    T, D = x.shape
    br = _tile(T, NORM_ROWS, 8)
    return pl.pallas_call(
        _rmsnorm_kernel,
        grid=(T // br,),
        in_specs=[pl.BlockSpec((br, D), lambda i: (i, 0)), pl.BlockSpec((1, D), lambda i: (0, 0))],
        out_specs=pl.BlockSpec((br, D), lambda i: (i, 0)),
        out_shape=jax.ShapeDtypeStruct((T, D), out_dtype),
        compiler_params=_params("parallel"),
        name="rmsnorm",
    )(x, g.reshape(1, D).astype(F32))


def _mm_kernel(*refs, nk, has_res):
    if has_res:
        a_ref, w_ref, r_ref, o_ref = refs[:4]
    else:
        a_ref, w_ref, o_ref = refs[:3]
        r_ref = None
    def run(first):
        a = a_ref[...]
        for c0 in range(0, o_ref.shape[1], MM_BN):
            cols = slice(c0, min(c0 + MM_BN, o_ref.shape[1]))
            part = jnp.dot(a, w_ref[:, cols], preferred_element_type=F32)
            if not first:
                o_ref[:, cols] += part
            else:
                total = part + r_ref[:, cols] if has_res else part
                o_ref[:, cols] = total.astype(o_ref.dtype)

    if nk == 1:
        run(True)
        return
    k = pl.program_id(2)
    pl.when(k == 0)(functools.partial(run, True))
    pl.when(k > 0)(functools.partial(run, False))


def matmul(a, w, res=None, out_dtype=BF16, bm=MM_BM, bn=MM_BN, bk=MM_BK):
    M, K = a.shape
    N = w.shape[1]
    bm, bn, bk = _tile(M, bm, 8), _tile(N, bn), _tile(K, bk)
    nk = K // bk
    assert nk == 1 or out_dtype == F32
    in_specs = [pl.BlockSpec((bm, bk), lambda i, j, k: (i, k)), pl.BlockSpec((bk, bn), lambda i, j, k: (k, j))]
    args = [a, w]
    if res is not None:
        in_specs.append(pl.BlockSpec((bm, bn), lambda i, j, k: (i, j)))
        args.append(res)
    return pl.pallas_call(
        functools.partial(_mm_kernel, nk=nk, has_res=res is not None),
        grid=(M // bm, N // bn, nk),
        in_specs=in_specs,
        out_specs=pl.BlockSpec((bm, bn), lambda i, j, k: (i, j)),
        out_shape=jax.ShapeDtypeStruct((M, N), out_dtype),
        compiler_params=_params("parallel", "parallel", "arbitrary"),
        name="matmul",
    )(*args)


def _headnorm_kernel(x_ref, g_ref, o_ref, *, dh):
    width = x_ref.shape[-1]
    g = g_ref[0]
    for c0 in range(0, width, dh):
        x = x_ref[:, c0:c0 + dh].astype(F32)
        inv = lax.rsqrt(jnp.mean(x * x, axis=-1, keepdims=True) + EPS)
        o_ref[:, c0:c0 + dh] = (x * inv * g).astype(o_ref.dtype)


def headnorm(x, gains, col_block0, width):
    T = x.shape[0]
    n, dh = gains.shape
    br = _tile(T, NORM_ROWS, 8)
    return pl.pallas_call(
        functools.partial(_headnorm_kernel, dh=dh),
        grid=(T // br, n),
        in_specs=[
            pl.BlockSpec((br, width), lambda i, j: (i, col_block0 + j)),
            pl.BlockSpec((1, 1, dh), lambda i, j: (j, 0, 0)),
        ],
        out_specs=pl.BlockSpec((br, width), lambda i, j: (i, j)),
        out_shape=jax.ShapeDtypeStruct((T, n * width), BF16),
        compiler_params=_params("parallel", "parallel"),
        name="headnorm",
    )(x, gains.reshape(n, 1, dh).astype(F32))


def _conformer_kernel(a_ref, g_ref, ha_ref, hg_ref, w_ref, b_ref, lg_ref, lb_ref, o_ref, buf_ref, ph_ref, y_ref,
                      *, blocks_per_seq):
    bs, C = a_ref.shape
    halo = ha_ref.shape[0]
    kw = w_ref.shape[0]
    first = (pl.program_id(0) % blocks_per_seq) == 0
    buf_ref[halo:halo + bs, :] = a_ref[...].astype(F32) * jax.nn.sigmoid(g_ref[...].astype(F32))
    hc = ha_ref[...].astype(F32) * jax.nn.sigmoid(hg_ref[...].astype(F32))
    buf_ref[0:halo, :] = jnp.where(first, 0.0, hc)
    off = halo - (kw - 1)
    pr = ph_ref.shape[1]
    for b in range(1, 8):
        ph_ref[b - 1] = buf_ref[b:b + pr, :]
    rs, cs = min(bs, 64), min(C, 512)
    for r0 in range(0, bs, rs):
        for c0 in range(0, C, cs):
            acc = jnp.broadcast_to(b_ref[:, c0:c0 + cs], (rs, cs))
            for k in range(kw):
                a8, b = divmod(off + k, 8)
                src = buf_ref if b == 0 else ph_ref.at[b - 1]
                acc = acc + w_ref[k:k + 1, c0:c0 + cs] * src[8 * a8 + r0:8 * a8 + r0 + rs, c0:c0 + cs]
            y_ref[r0:r0 + rs, c0:c0 + cs] = acc
    y = y_ref[...]
    yc = y - jnp.mean(y, axis=-1, keepdims=True)
    z = yc * lax.rsqrt(jnp.mean(yc * yc, axis=-1, keepdims=True) + EPS) * lg_ref[...] + lb_ref[...]
    o_ref[...] = _silu(z).astype(o_ref.dtype)


def conformer_conv(u, conv_w, conv_b, ln_g, ln_b, seq):
    T = u.shape[0]
    kw, C = conv_w.shape
    bs = _tile(seq, CONF_ROWS, CONF_HALO)
    halo = CONF_HALO
    assert kw - 1 <= halo and bs % halo == 0
    hb = bs // halo
    row = lambda v: v.reshape(1, C).astype(F32)
    return pl.pallas_call(
        functools.partial(_conformer_kernel, blocks_per_seq=seq // bs),
        grid=(T // bs,),
        in_specs=[
            pl.BlockSpec((bs, C), lambda i: (i, 0)),
            pl.BlockSpec((bs, C), lambda i: (i, 1)),
            pl.BlockSpec((halo, C), lambda i: (jnp.maximum(i * hb - 1, 0), 0)),
            pl.BlockSpec((halo, C), lambda i: (jnp.maximum(i * hb - 1, 0), 1)),
            pl.BlockSpec((kw, C), lambda i: (0, 0)),
            pl.BlockSpec((1, C), lambda i: (0, 0)),
            pl.BlockSpec((1, C), lambda i: (0, 0)),
            pl.BlockSpec((1, C), lambda i: (0, 0)),
        ],
        out_specs=pl.BlockSpec((bs, C), lambda i: (i, 0)),
        out_shape=jax.ShapeDtypeStruct((T, C), BF16),
        scratch_shapes=[pltpu.VMEM((bs + halo, C), F32), pltpu.VMEM((7, bs + halo - 8, C), F32),
                        pltpu.VMEM((bs, C), F32)],
        compiler_params=_params("parallel"),
        name="conformer_conv",
    )(u, u, u, u, conv_w.astype(F32), row(conv_b), row(ln_g), row(ln_b))


def _qk_prep_kernel(x_ref, g_ref, base_ref, coef_ref, o_ref, *, dh, slopes2, bq):
    br, width = x_ref.shape
    g = g_ref[0]
    period = jnp.where(pl.program_id(1) == 0, bq, 2 * bq)
    pos = ((pl.program_id(0) * br) % period + lax.broadcasted_iota(jnp.int32, (br, 1), 0)).astype(F32)
    lane = lax.broadcasted_iota(jnp.int32, (br, dh), 1)
    for grp in range(width // dh):
        x = x_ref[:, grp * dh:(grp + 1) * dh].astype(F32)
        inv = lax.rsqrt(jnp.mean(x * x, axis=-1, keepdims=True) + EPS)
        o_ref[:, 2 * grp * dh:(2 * grp + 1) * dh] = (x * inv * g).astype(o_ref.dtype)
        hi, mid, lo = (p.astype(F32) for p in _split3(pos * slopes2[grp // 2]))
        pieces = jnp.where(lane % 3 == 0, hi, jnp.where(lane % 3 == 1, mid, lo))
        ext = base_ref[0] + coef_ref[0] * pieces
        o_ref[:, (2 * grp + 1) * dh:(2 * grp + 2) * dh] = ext.astype(o_ref.dtype)


def qk_prep(u, gains, col_block0, width, slopes2, bq, shift):
    T = u.shape[0]
    dh = gains.shape[-1]
    br = _tile(bq, NORM_ROWS, 8)
    lane = jnp.arange(dh)
    sh = jnp.stack([p.astype(F32) for p in _split3(-shift.astype(F32))])
    base_q = jnp.where(lane < 3, 1.0, 0.0).at[6:9].set(sh)
    base_k = jnp.where((lane >= 3) & (lane < 9), 1.0, 0.0)
    coef_q = jnp.where((lane >= 3) & (lane < 6), -1.0, 0.0)
    coef_k = jnp.where(lane < 3, 1.0, 0.0)
    vec = lambda a, b: jnp.stack([a, b]).astype(F32).reshape(2, 1, dh)
    spec = pl.BlockSpec((1, 1, dh), lambda i, j: (j, 0, 0))
    return pl.pallas_call(
        functools.partial(_qk_prep_kernel, dh=dh, slopes2=slopes2, bq=bq),
        grid=(T // br, 2),
        in_specs=[pl.BlockSpec((br, width), lambda i, j: (i, col_block0 + j)), spec, spec, spec],
        out_specs=pl.BlockSpec((br, 2 * width), lambda i, j: (i, j)),
        out_shape=jax.ShapeDtypeStruct((T, 4 * width), BF16),
        compiler_params=_params("parallel", "parallel"),
        name="qk_prep",
    )(u, gains.reshape(2, 1, dh).astype(F32), vec(base_q, base_k), vec(coef_q, coef_k))


def _diff_attn_bounded_kernel(lam_ref, slope_ref, q_ref, k_ref, v_ref, sub_ref, o_ref, acc_ref, l_ref, *, out_scale):
    bq = q_ref.shape[0]
    dk = q_ref.shape[1] // 2
    qi = pl.program_id(2)
    slope2 = slope_ref[0, pl.program_id(1)]
    visible = lax.broadcasted_iota(jnp.int32, (bq, bq), 1) <= lax.broadcasted_iota(jnp.int32, (bq, bq), 0)
    qs = [q_ref[:, c * dk:(c + 1) * dk] for c in range(2)]

    def step(blk, width, masked):
        cj = slope2 * ((blk // 2) * (2 * bq) - qi * bq).astype(F32)
        rows = pl.ds(pl.multiple_of(blk * bq, bq), width)
        vb = v_ref[rows, :]
        for c in range(2):
            s = lax.dot_general(qs[c], k_ref[rows, c * dk:(c + 1) * dk], (((1,), (1,)), ((), ())),
                                preferred_element_type=F32) + cj
            if masked:
                s = jnp.where(visible, s, -jnp.inf)
            p = jnp.exp2(s)
            l_ref[c] += sum(p[:, t:t + LANE] for t in range(0, width, LANE))
            acc_ref[c] += jnp.dot(p.astype(BF16), vb, preferred_element_type=F32)

    acc_ref[...] = jnp.zeros_like(acc_ref)
    l_ref[...] = jnp.zeros_like(l_ref)

    def pair(i, carry):
        step(2 * i, 2 * bq, masked=False)
        return carry

    lax.fori_loop(0, qi // 2, pair, 0)

    @pl.when(qi % 2 == 1)
    def _():
        step(qi - 1, bq, masked=False)

    step(qi, bq, masked=True)
    l0 = jnp.sum(l_ref[0], axis=-1, keepdims=True)
    l1 = jnp.sum(l_ref[1], axis=-1, keepdims=True)
    o = acc_ref[0] / l0 - lam_ref[0, 0] * (acc_ref[1] / l1)
    o = o * lax.rsqrt(jnp.mean(o * o, axis=-1, keepdims=True) + EPS) * sub_ref[...] * out_scale
    o_ref[...] = o.astype(o_ref.dtype)


def _diff_attn_kernel(lam_ref, slope_ref, q_ref, k_ref, v_ref, sub_ref, o_ref, acc_ref, *, out_scale):
    bq = q_ref.shape[0]
    dk = q_ref.shape[1] // 2
    bk = bq
    qi = pl.program_id(2)
    slope2 = slope_ref[0, pl.program_id(1)]
    visible = lax.broadcasted_iota(jnp.int32, (bq, bk), 1) <= lax.broadcasted_iota(jnp.int32, (bq, bk), 0)
    qs = [q_ref[:, c * dk:(c + 1) * dk] for c in range(2)]

    def step(j, carry, masked):
        cj = slope2 * (((j // 2) * 2 - qi) * bk).astype(F32)
        rows = pl.ds(pl.multiple_of(j * bk, bk), bk)
        vb = v_ref[rows, :]
        out = []
        for c in range(2):
            m, l = carry[2 * c], carry[2 * c + 1]
            s = lax.dot_general(qs[c], k_ref[rows, c * dk:(c + 1) * dk], (((1,), (1,)), ((), ())),
                                preferred_element_type=F32)
            if masked:
                s = jnp.where(visible, s, -jnp.inf)
            m_new = jnp.maximum(m, jnp.max(s, axis=-1, keepdims=True) + cj)
            alpha = jnp.exp2(m - m_new)
            p = jnp.exp2(s - (m_new - cj))
            l = alpha * l + jnp.sum(p, axis=-1, keepdims=True)
            acc_ref[c] = alpha * acc_ref[c] + jnp.dot(p.astype(BF16), vb, preferred_element_type=F32)
            out += [m_new, l]
        return tuple(out)

    acc_ref[...] = jnp.zeros_like(acc_ref)
    neg = jnp.full((bq, 1), -jnp.inf, F32)
    zero = jnp.zeros((bq, 1), F32)
    carry = lax.fori_loop(0, qi, functools.partial(step, masked=False), (neg, zero, neg, zero))
    _, l0, _, l1 = step(qi, carry, masked=True)
    o = acc_ref[0] / l0 - lam_ref[0, 0] * (acc_ref[1] / l1)
    o = o * lax.rsqrt(jnp.mean(o * o, axis=-1, keepdims=True) + EPS) * sub_ref[...] * out_scale
    o_ref[...] = o.astype(o_ref.dtype)


def _alibi_slopes2(n_heads):
    return tuple(2.0 ** (-8.0 * (h + 1) / n_heads) * math.log2(math.e) for h in range(n_heads))


def diff_attention(qk, u, v_col_block0, lam, subln, lambda_init, batch, seq, n_heads, bq, bounded):
    T = qk.shape[0]
    dv = subln.shape[-1]
    nq = seq // bq
    slopes2 = jnp.asarray(_alibi_slopes2(n_heads), F32).reshape(1, n_heads)
    body = _diff_attn_bounded_kernel if bounded else _diff_attn_kernel
    scratch = [pltpu.VMEM((2, bq, dv), F32)] + ([pltpu.VMEM((2, bq, LANE), F32)] if bounded else [])
    return pl.pallas_call(
        functools.partial(body, out_scale=1.0 - lambda_init),
        grid=(batch, n_heads, nq),
        in_specs=[
            pl.BlockSpec(memory_space=pltpu.SMEM),
            pl.BlockSpec(memory_space=pltpu.SMEM),
            pl.BlockSpec((bq, 2 * dv), lambda b, h, i: (b * nq + i, h)),
            pl.BlockSpec((seq, 2 * dv), lambda b, h, i: (b, n_heads + h)),
            pl.BlockSpec((seq, dv), lambda b, h, i: (b, v_col_block0 + h)),
            pl.BlockSpec((1, dv), lambda b, h, i: (0, 0)),
        ],
        out_specs=pl.BlockSpec((bq, dv), lambda b, h, i: (b * nq + i, h)),
        out_shape=jax.ShapeDtypeStruct((T, n_heads * dv), BF16),
        scratch_shapes=scratch,
        compiler_params=_params("parallel", "parallel", "arbitrary"),
        name="diff_attention_bounded" if bounded else "diff_attention",
    )(lam.reshape(1, 1).astype(F32), slopes2, qk, qk, u, subln.reshape(1, dv).astype(F32))


def _sconv_kernel(x_ref, hx_ref, w_ref, b_ref, o_ref, buf_ref, *, blocks_per_seq):
    bs, _ = x_ref.shape
    halo = hx_ref.shape[0]
    kw = w_ref.shape[0]
    first = (pl.program_id(1) % blocks_per_seq) == 0
    buf_ref[halo:halo + bs, :] = x_ref[...].astype(F32)
    buf_ref[0:halo, :] = jnp.where(first, 0.0, hx_ref[...].astype(F32))
    off = halo - (kw - 1)
    acc = b_ref[...] + w_ref[0:1, :] * buf_ref[off:off + bs, :]
    for k in range(1, kw):
        acc = acc + w_ref[k:k + 1, :] * buf_ref[off + k:off + k + bs, :]
    o_ref[...] = _silu(acc).astype(o_ref.dtype)


def ssd_conv(xbc, conv_w, conv_b, seq):
    T, C = xbc.shape
    kw = conv_w.shape[0]
    bs = _tile(seq, SCONV_ROWS, SCONV_HALO)
    halo = SCONV_HALO
    bc = _tile(C, 1024)
    hb = bs // halo
    return pl.pallas_call(
        functools.partial(_sconv_kernel, blocks_per_seq=seq // bs),
        grid=(C // bc, T // bs),
        in_specs=[
            pl.BlockSpec((bs, bc), lambda j, i: (i, j)),
            pl.BlockSpec((halo, bc), lambda j, i: (jnp.maximum(i * hb - 1, 0), j)),
            pl.BlockSpec((kw, bc), lambda j, i: (0, j)),
            pl.BlockSpec((1, bc), lambda j, i: (0, j)),
        ],
        out_specs=pl.BlockSpec((bs, bc), lambda j, i: (i, j)),
        out_shape=jax.ShapeDtypeStruct((T, C), BF16),
        scratch_shapes=[pltpu.VMEM((bs + halo, bc), F32)],
        compiler_params=_params("parallel", "parallel"),
        name="ssd_conv",
    )(xbc, xbc, conv_w.astype(F32), conv_b.reshape(1, C).astype(F32))


def _split3(x):
    hi = x.astype(BF16)
    r = x - hi.astype(F32)
    mid = r.astype(BF16)
    lo = (r - mid.astype(F32)).astype(BF16)
    return hi, mid, lo


def _softplus(x):
    return jnp.maximum(x, 0.0) + jnp.log1p(jnp.exp(-jnp.abs(x)))


def _ssd_kernel(x_ref, z_ref, b_ref, c_ref, dt_ref, dtt_ref, a_ref, at_ref, db_ref, dbt_ref, e_ref, dsk_ref, nw_ref,
                o_ref, state_ref, *, hpg, hd, L):
    hpv = LANE // hd

    @pl.when(pl.program_id(2) == 0)
    def _():
        state_ref[...] = jnp.zeros_like(state_ref)

    ri = lax.broadcasted_iota(jnp.int32, (L, L), 0)
    ci = lax.broadcasted_iota(jnp.int32, (L, L), 1)
    causal = ci <= ri
    tril = causal.astype(BF16)
    triu = (ri <= ci).astype(BF16)
    lane_head = lax.broadcasted_iota(jnp.int32, (L, LANE), 1) // hd

    def expand(v):
        return jnp.dot(jnp.concatenate(_split3(v), axis=1), e_ref[...], preferred_element_type=F32)

    for r0 in range(0, x_ref.shape[0], L):
        rows = slice(r0, r0 + L)
        dt = _softplus(dt_ref[0, 0, rows, :] + db_ref[0])
        dtt = _softplus(dtt_ref[0, 0, :, rows] + dbt_ref[0])
        a = dt * a_ref[0]
        at = dtt * at_ref[0]
        acum = sum(jnp.dot(tril, p, preferred_element_type=F32) for p in _split3(a))
        acumt = sum(jnp.dot(p, triu, preferred_element_type=F32) for p in _split3(at))
        a_last = acum[L - 1:L, :]
        ea_x = expand(jnp.exp(acum))
        wst_x = expand(jnp.exp(a_last - acum) * dt)

        bm = b_ref[rows, :]
        cm = c_ref[rows, :]
        x = x_ref[rows, :]
        x32 = x.astype(F32)
        state = state_ref[...]
        y = jnp.dot(cm, state.astype(BF16), preferred_element_type=F32) * ea_x + dsk_ref[...] * x32
        xw = (x32 * wst_x).astype(BF16)
        upd = lax.dot_general(bm, xw, (((0,), (0,)), ((), ())), preferred_element_type=F32)
        state_ref[...] = state * ea_x[L - 1:L, :] + upd

        cb = lax.dot_general(cm, bm, (((1,), (1,)), ((), ())), preferred_element_type=F32)
        cbm = jnp.where(causal, cb, 0.0)
        groups = []
        for g0 in range(0, hpg, hpv):
            xg = x[:, g0 * hd:(g0 + hpv) * hd]
            yg = None
            for r in range(hpv):
                j = g0 + r
                seg = jnp.minimum(acum[:, j:j + 1] - acumt[j:j + 1, :], 0.0)
                mj = (cbm * jnp.exp(seg) * dtt[j:j + 1, :]).astype(BF16)
                d = jnp.dot(mj, xg, preferred_element_type=F32)
                yg = d if yg is None else jnp.where(lane_head == r, d, yg)
            groups.append(yg)
        y = y + jnp.concatenate(groups, axis=1)

        g = y * _silu(z_ref[rows, :].astype(F32))
        g = g * lax.rsqrt(jnp.mean(g * g, axis=-1, keepdims=True) + EPS) * nw_ref[...]
        o_ref[rows, :] = g.astype(o_ref.dtype)


def ssd_scan(xbc, z, dt_raw, dt_bias, a_log, d_skip, norm_w, batch, seq, d_inner):
    T = xbc.shape[0]
    H = dt_raw.shape[-1]
    N = SSM_STATE
    G = (xbc.shape[1] - d_inner) // (2 * N)
    hpg = H // G
    hd = d_inner // H
    gw = hpg * hd
    L = _tile(seq, SSM_CHUNK, 8)
    lb = _tile(seq, SSM_BLOCK_CHUNKS * L, L)
    nc = seq // lb
    assert gw % LANE == 0 and gw % N == 0 and LANE % hd == 0
    b0 = d_inner // N
    expand = jnp.tile(jnp.repeat(jnp.eye(hpg, dtype=BF16), hd, axis=1), (3, 1))
    dtg = dt_raw.reshape(batch, seq, G, hpg).transpose(0, 2, 1, 3)
    dttg = dtg.transpose(0, 1, 3, 2)
    a = -jnp.exp(a_log.astype(F32)).reshape(G, 1, hpg)
    db = dt_bias.astype(F32).reshape(G, 1, hpg)
    dsk = jnp.repeat(d_skip.astype(F32), hd).reshape(1, d_inner)
    row = lambda b, g, c: (b * nc + c, g)
    return pl.pallas_call(
        functools.partial(_ssd_kernel, hpg=hpg, hd=hd, L=L),
        grid=(batch, G, nc),
        in_specs=[
            pl.BlockSpec((lb, gw), row),
            pl.BlockSpec((lb, gw), row),
            pl.BlockSpec((lb, N), lambda b, g, c: (b * nc + c, b0 + g)),
            pl.BlockSpec((lb, N), lambda b, g, c: (b * nc + c, b0 + G + g)),
            pl.BlockSpec((1, 1, lb, hpg), lambda b, g, c: (b, g, c, 0)),
            pl.BlockSpec((1, 1, hpg, lb), lambda b, g, c: (b, g, 0, c)),
            pl.BlockSpec((1, 1, hpg), lambda b, g, c: (g, 0, 0)),
            pl.BlockSpec((1, hpg, 1), lambda b, g, c: (g, 0, 0)),
            pl.BlockSpec((1, 1, hpg), lambda b, g, c: (g, 0, 0)),
            pl.BlockSpec((1, hpg, 1), lambda b, g, c: (g, 0, 0)),
            pl.BlockSpec((3 * hpg, gw), lambda b, g, c: (0, 0)),
            pl.BlockSpec((1, gw), lambda b, g, c: (0, g)),
            pl.BlockSpec((1, gw), lambda b, g, c: (0, g)),
        ],
        out_specs=pl.BlockSpec((lb, gw), row),
        out_shape=jax.ShapeDtypeStruct((T, d_inner), BF16),
        scratch_shapes=[pltpu.VMEM((N, gw), F32)],
        compiler_params=_params("parallel", "parallel", "arbitrary"),
        name="ssd_scan",
    )(xbc, z, xbc, xbc, dtg, dttg, a, a.reshape(G, hpg, 1), db, db.reshape(G, hpg, 1), expand, dsk,
      norm_w.reshape(1, d_inner).astype(F32))


def _cross_kernel(h_ref, gx_ref, wq_ref, qn_ref, k_ref, v_ref, wo_ref, gf_ref, ho_ref, xn_ref, *, dh):
    h = h_ref[...]
    hn = (h * lax.rsqrt(jnp.mean(h * h, axis=-1, keepdims=True) + EPS) * gx_ref[...]).astype(BF16)
    q = jnp.dot(hn, wq_ref[...], preferred_element_type=F32)
    xd = q.shape[-1]
    heads = []
    for c0 in range(0, xd, dh):
        qh = q[:, c0:c0 + dh]
        qh = qh * lax.rsqrt(jnp.mean(qh * qh, axis=-1, keepdims=True) + EPS) * qn_ref[...]
        s = lax.dot_general(qh.astype(BF16), k_ref[0, :, c0:c0 + dh], (((1,), (1,)), ((), ())),
                            preferred_element_type=F32)
        s = s - jnp.max(s, axis=-1, keepdims=True)
        p = jnp.exp(s)
        p = p / jnp.sum(p, axis=-1, keepdims=True)
        heads.append(jnp.dot(p.astype(BF16), v_ref[0, :, c0:c0 + dh], preferred_element_type=F32))
    o = jnp.concatenate(heads, axis=-1).astype(BF16)
    hnew = h + jnp.dot(o, wo_ref[...], preferred_element_type=F32)
    ho_ref[...] = hnew
    xn = hnew * lax.rsqrt(jnp.mean(hnew * hnew, axis=-1, keepdims=True) + EPS) * gf_ref[...]
    xn_ref[...] = xn.astype(xn_ref.dtype)


def cross_attention(h, g_cross, wq, q_norm, k, v, wo, g_ffn, seq):
    T, D = h.shape
    X = wq.shape[1]
    M = k.shape[1]
    dh = q_norm.shape[-1]
    bm = _tile(seq, CROSS_ROWS, 8)
    per_seq = seq // bm
    const = lambda i: (0, 0)
    resident = dict(pipeline_mode=pl.Buffered(1))
    return pl.pallas_call(
        functools.partial(_cross_kernel, dh=dh),
        grid=(T // bm,),
        in_specs=[
            pl.BlockSpec((bm, D), lambda i: (i, 0)),
            pl.BlockSpec((1, D), const),
            pl.BlockSpec((D, X), const, **resident),
            pl.BlockSpec((1, dh), const),
            pl.BlockSpec((1, M, X), lambda i: (i // per_seq, 0, 0)),
            pl.BlockSpec((1, M, X), lambda i: (i // per_seq, 0, 0)),
            pl.BlockSpec((X, D), const, **resident),
            pl.BlockSpec((1, D), const),
        ],
        out_specs=[pl.BlockSpec((bm, D), lambda i: (i, 0)), pl.BlockSpec((bm, D), lambda i: (i, 0))],
        out_shape=[jax.ShapeDtypeStruct((T, D), F32), jax.ShapeDtypeStruct((T, D), BF16)],
        compiler_params=_params("parallel"),
        name="cross_attention",
    )(h, g_cross.reshape(1, D).astype(F32), wq, q_norm.reshape(1, dh).astype(F32), k, v, wo,
      g_ffn.reshape(1, D).astype(F32))


def _ffn_in_kernel(x_ref, wa_ref, wb_ref, cw_ref, cb_ref, o_ref, buf_ref, *, tiles_per_seq):
    bm = x_ref.shape[0]
    kw = cw_ref.shape[0]
    first = (pl.program_id(1) % tiles_per_seq) == 0
    @pl.when(first)
    def _():
        buf_ref[0:8, :] = jnp.zeros((8, buf_ref.shape[1]), F32)

    @pl.when(jnp.logical_not(first))
    def _():
        buf_ref[0:8, :] = buf_ref[bm:bm + 8, :]

    x = x_ref[...]
    buf_ref[8:8 + bm, :] = jnp.dot(x, wa_ref[...], preferred_element_type=F32)
    b = jnp.dot(x, wb_ref[...], preferred_element_type=F32)
    off = 8 - (kw - 1)
    acc = cb_ref[...] + cw_ref[0:1, :] * buf_ref[off:off + bm, :]
    for k in range(1, kw):
        acc = acc + cw_ref[k:k + 1, :] * buf_ref[off + k:off + k + bm, :]
    o_ref[...] = (_silu(acc) * b).astype(o_ref.dtype)


def ffn_in(xn, wa, wb, conv_w, conv_b, seq):
    T, D = xn.shape
    F = wa.shape[1]
    bm = _tile(seq, FFN_BM, 8)
    bn = _tile(F, FFN_BN)
    return pl.pallas_call(
        functools.partial(_ffn_in_kernel, tiles_per_seq=seq // bm),
        grid=(F // bn, T // bm),
        in_specs=[
            pl.BlockSpec((bm, D), lambda j, i: (i, 0)),
            pl.BlockSpec((D, bn), lambda j, i: (0, j)),
            pl.BlockSpec((D, bn), lambda j, i: (0, j)),
            pl.BlockSpec((conv_w.shape[0], bn), lambda j, i: (0, j)),
            pl.BlockSpec((1, bn), lambda j, i: (0, j)),
        ],
        out_specs=pl.BlockSpec((bm, bn), lambda j, i: (i, j)),
        out_shape=jax.ShapeDtypeStruct((T, F), BF16),
        scratch_shapes=[pltpu.VMEM((bm + 8, bn), F32)],
        compiler_params=_params("parallel", "arbitrary"),
        name="ffn_in",
    )(xn, wa, wb, conv_w, conv_b)


def _pad_cols(w, n):
    return jnp.pad(w, ((0, 0), (0, n - w.shape[1])))


def kernel(x, mem, norm_mix, norm_cross, norm_mem, norm_ffn, xq_w, xk_w, xv_w, xo_w, xq_norm, xk_norm, ffn_in_w, ffn_conv_w, ffn_conv_b, ffn_out_w, ev_in_w, ev_out_w, cv_conv_w, cv_conv_b, cv_ln_g, cv_ln_b, da_q_norm, da_k_norm, da_lq1, da_lk1, da_lq2, da_lk2, da_subln, m_in_w, m_conv_w, m_conv_b, m_dt_bias, m_A_log, m_D, m_norm, m_out_w):
    batch, seq, D = x.shape
    depth = norm_mix.shape[0]
    T = batch * seq
    mem_len = mem.shape[1]
    x_dim = xq_w.shape[-1]
    x_dh = xq_norm.shape[-1]
    d_ff = ffn_conv_w.shape[-1]
    f_pad = -(-d_ff // FFN_PAD) * FFN_PAD
    conv_ch = cv_conv_w.shape[-1]
    da_dh = da_q_norm.shape[-1]
    da_dv = da_subln.shape[-1]
    da_qk = (ev_in_w.shape[-1] - 2 * conv_ch) // 3
    da_heads = da_qk // (2 * da_dh)
    d_inner = m_norm.shape[-1]
    ssm_heads = m_dt_bias.shape[-1]
    assert da_dv == 2 * da_dh and conv_ch % da_qk == 0 and da_qk % da_dv == 0

    h = x.reshape(T, D)
    mem2 = mem.reshape(batch * mem_len, D)

    for i in range(depth):
        xn = rmsnorm(h, norm_mix[i])
        if i % 2 == 0:
            e = i // 2
            lambda_init = 0.8 - 0.6 * math.exp(-0.3 * i)
            u = matmul(xn, ev_in_w[e].astype(BF16))
            c = conformer_conv(u, cv_conv_w[e], cv_conv_b[e], cv_ln_g[e], cv_ln_b[e], seq)
            gains = jnp.stack([da_q_norm[e].astype(F32) * (da_dh ** -0.5 * math.log2(math.e)),
                               da_k_norm[e].astype(F32)])
            bq = _tile(seq, ATT_BQ, 8)
            assert seq % (2 * bq) == 0
            shift = 1.02 * da_dh * jnp.max(jnp.abs(gains[0])) * jnp.max(jnp.abs(gains[1])) + 0.1
            qk = qk_prep(u, gains, 2 * conv_ch // da_qk, da_qk, _alibi_slopes2(da_heads), bq, shift)
            lam = (jnp.exp(jnp.sum(da_lq1[e].astype(F32) * da_lk1[e].astype(F32)))
                   - jnp.exp(jnp.sum(da_lq2[e].astype(F32) * da_lk2[e].astype(F32))) + lambda_init)
            attn = functools.partial(diff_attention, qk, u, (2 * conv_ch + 2 * da_qk) // da_dv, lam, da_subln[e],
                                     lambda_init, batch, seq, da_heads, bq)
            o = lax.cond(shift <= ATT_MAX_SHIFT, lambda: attn(True), lambda: attn(False))
            y = jnp.concatenate([c, o], axis=-1)
            h = matmul(y, ev_out_w[e].astype(BF16), res=h, out_dtype=F32)
        else:
            o_ = i // 2
            w_in = m_in_w[o_]
            z = matmul(xn, w_in[:, :d_inner].astype(BF16))
            xbc = matmul(xn, w_in[:, d_inner:w_in.shape[1] - ssm_heads].astype(BF16))
            dt_raw = matmul(xn, w_in[:, w_in.shape[1] - ssm_heads:].astype(BF16), out_dtype=F32)
            xbc = ssd_conv(xbc, m_conv_w[o_], m_conv_b[o_], seq)
            g = ssd_scan(xbc, z, dt_raw, m_dt_bias[o_], m_A_log[o_], m_D[o_], m_norm[o_], batch, seq, d_inner)
            h = matmul(g, m_out_w[o_].astype(BF16), res=h, out_dtype=F32, bn=MM_WIDE_BN, bk=MM_WIDE_BK)

        memn = rmsnorm(mem2, norm_mem[i])
        kv = matmul(memn, jnp.concatenate([xk_w[i], xv_w[i]], axis=1).astype(BF16), out_dtype=F32)
        kx = headnorm(kv, xk_norm[i].reshape(1, x_dh), 0, x_dim).reshape(batch, mem_len, x_dim)
        vx = kv[:, x_dim:].astype(BF16).reshape(batch, mem_len, x_dim)
        h, xn = cross_attention(h, norm_cross[i], xq_w[i].astype(BF16), xq_norm[i].astype(F32) * (x_dh ** -0.5),
                                kx, vx, xo_w[i].astype(BF16), norm_ffn[i], seq)

        wa = _pad_cols(ffn_in_w[i][:, :d_ff], f_pad).astype(BF16)
        wb = _pad_cols(ffn_in_w[i][:, d_ff:], f_pad).astype(BF16)
        cw = _pad_cols(ffn_conv_w[i].astype(F32), f_pad)
        cb = _pad_cols(ffn_conv_b[i].astype(F32).reshape(1, d_ff), f_pad)
        act = ffn_in(xn, wa, wb, cw, cb, seq)
        w_out = jnp.pad(ffn_out_w[i], ((0, f_pad - d_ff), (0, 0))).astype(BF16)
        h = matmul(act, w_out, res=h, out_dtype=F32, bn=MM_WIDE_BN, bk=MM_WIDE_BK)
    return h.reshape(batch, seq, D)
```

```python
import functools
import math

import jax
import jax.numpy as jnp
from jax import lax
from jax.experimental import pallas as pl
from jax.experimental.pallas import tpu as pltpu

F32 = jnp.float32
BF16 = jnp.bfloat16
EPS = 1e-6

V7X_VMEM_LIMIT_BYTES = 56 * 1024 * 1024
LANE = 128

SSM_STATE = 128
SSM_CHUNK = 128
SSM_BLOCK_CHUNKS = 4
FFN_PAD = 1024

MM_BM = 1024
MM_BN = 1024
MM_BK = 4096
MM_WIDE_BN = 2048
MM_WIDE_BK = 1024
NORM_ROWS = 256
CAST_ROWS = 2048
CAST_COLS = 1024
CAST_PAD_ROWS = 256
CONF_ROWS = 128
CONF_HALO = 32
ATT_BQ = 512
ATT_MAX_SHIFT = 48.0
CROSS_ROWS = 256
FFN_BM = 1024
FFN_BN = 512
SCONV_ROWS = 256
SCONV_HALO = 16


def _tile(dim, pref, align=LANE):
    if dim <= pref:
        return dim
    t = (pref // align) * align
    while t >= align:
        if dim % t == 0:
            return t
        t -= align
    return dim


def _params(*sem):
    return pltpu.CompilerParams(dimension_semantics=sem, vmem_limit_bytes=V7X_VMEM_LIMIT_BYTES)


def _silu(x):
    return x * jax.nn.sigmoid(x)


def _rmsnorm_kernel(x_ref, g_ref, o_ref):
    x = x_ref[...].astype(F32)
    inv = lax.rsqrt(jnp.mean(x * x, axis=-1, keepdims=True) + EPS)
    o_ref[...] = (x * inv * g_ref[...]).astype(o_ref.dtype)


def rmsnorm(x, g, out_dtype=BF16):
    T, D = x.shape
    br = _tile(T, NORM_ROWS, 8)
    return pl.pallas_call(
        _rmsnorm_kernel,
        grid=(T // br,),
        in_specs=[pl.BlockSpec((br, D), lambda i: (i, 0)), pl.BlockSpec((1, D), lambda i: (0, 0))],
        out_specs=pl.BlockSpec((br, D), lambda i: (i, 0)),
        out_shape=jax.ShapeDtypeStruct((T, D), out_dtype),
        compiler_params=_params("parallel"),
        name="rmsnorm",
    )(x, g.reshape(1, D).astype(F32))


def _mm_kernel(*refs, nk, has_res):
    if has_res:
        a_ref, w_ref, r_ref, o_ref = refs[:4]
    else:
        a_ref, w_ref, o_ref = refs[:3]
        r_ref = None
    def run(first):
        a = a_ref[...]
        for c0 in range(0, o_ref.shape[1], MM_BN):
            cols = slice(c0, min(c0 + MM_BN, o_ref.shape[1]))
            part = jnp.dot(a, w_ref[:, cols], preferred_element_type=F32)
            if not first:
                o_ref[:, cols] += part
            else:
                total = part + r_ref[:, cols] if has_res else part
                o_ref[:, cols] = total.astype(o_ref.dtype)

    if nk == 1:
        run(True)
        return
    k = pl.program_id(2)
    pl.when(k == 0)(functools.partial(run, True))
    pl.when(k > 0)(functools.partial(run, False))


def matmul(a, w, res=None, out_dtype=BF16, bm=MM_BM, bn=MM_BN, bk=MM_BK, w_col0=0, n=None):
    M, K = a.shape
    N = w.shape[1] if n is None else n
    bm, bn, bk = _tile(M, bm, 8), _tile(math.gcd(N, w_col0), bn), _tile(K, bk)
    nk = K // bk
    assert nk == 1 or out_dtype == F32
    assert w_col0 % bn == 0 and N % bn == 0
    jb = w_col0 // bn
    in_specs = [pl.BlockSpec((bm, bk), lambda i, j, k: (i, k)), pl.BlockSpec((bk, bn), lambda i, j, k: (k, jb + j))]
    args = [a, w]
    if res is not None:
        in_specs.append(pl.BlockSpec((bm, bn), lambda i, j, k: (i, j)))
        args.append(res)
    return pl.pallas_call(
        functools.partial(_mm_kernel, nk=nk, has_res=res is not None),
        grid=(M // bm, N // bn, nk),
        in_specs=in_specs,
        out_specs=pl.BlockSpec((bm, bn), lambda i, j, k: (i, j)),
        out_shape=jax.ShapeDtypeStruct((M, N), out_dtype),
        compiler_params=_params("parallel", "parallel", "arbitrary"),
        name="matmul",
    )(*args)


def _cast_kernel(x_ref, o_ref, *, n_valid, axis):
    t = pl.program_id(axis)

    @pl.when(t < n_valid)
    def _():
        o_ref[...] = x_ref[...].astype(o_ref.dtype)

    @pl.when(t >= n_valid)
    def _():
        o_ref[...] = jnp.zeros_like(o_ref)


def cast_cols(stack, layer, out_cols=None, halves=1, bc=CAST_COLS):
    _, K, N = stack.shape
    ncols = N // halves
    out_cols = ncols if out_cols is None else out_cols
    bc = _tile(ncols, bc)
    br = _tile(K, CAST_ROWS, 8)
    assert out_cols % bc == 0
    nv, no = ncols // bc, out_cols // bc
    return pl.pallas_call(
        functools.partial(_cast_kernel, n_valid=nv, axis=2),
        grid=(K // br, halves, no),
        in_specs=[pl.BlockSpec((None, br, bc), lambda r, h, c: (layer, r, h * nv + jnp.minimum(c, nv - 1)))],
        out_specs=pl.BlockSpec((br, bc), lambda r, h, c: (r, h * no + c)),
        out_shape=jax.ShapeDtypeStruct((K, halves * out_cols), BF16),
        compiler_params=_params("parallel", "parallel", "parallel"),
        name="cast_cols",
    )(stack)


def cast_rows(stack, layer, out_rows, br=CAST_PAD_ROWS):
    _, K, N = stack.shape
    br = _tile(math.gcd(K, out_rows), br, 8)
    bc = _tile(N, 4096)
    assert out_rows % br == 0 and K % br == 0
    nv = K // br
    return pl.pallas_call(
        functools.partial(_cast_kernel, n_valid=nv, axis=0),
        grid=(out_rows // br, N // bc),
        in_specs=[pl.BlockSpec((None, br, bc), lambda r, c: (layer, jnp.minimum(r, nv - 1), c))],
        out_specs=pl.BlockSpec((br, bc), lambda r, c: (r, c)),
        out_shape=jax.ShapeDtypeStruct((out_rows, N), BF16),
        compiler_params=_params("parallel", "parallel"),
        name="cast_rows",
    )(stack)


def _headnorm_kernel(x_ref, g_ref, o_ref, *, dh):
    width = x_ref.shape[-1]
    g = g_ref[0]
    for c0 in range(0, width, dh):
        x = x_ref[:, c0:c0 + dh].astype(F32)
        inv = lax.rsqrt(jnp.mean(x * x, axis=-1, keepdims=True) + EPS)
        o_ref[:, c0:c0 + dh] = (x * inv * g).astype(o_ref.dtype)


def headnorm(x, gains, col_block0, width):
    T = x.shape[0]
    n, dh = gains.shape
    br = _tile(T, NORM_ROWS, 8)
    return pl.pallas_call(
        functools.partial(_headnorm_kernel, dh=dh),
        grid=(T // br, n),
        in_specs=[
            pl.BlockSpec((br, width), lambda i, j: (i, col_block0 + j)),
            pl.BlockSpec((1, 1, dh), lambda i, j: (j, 0, 0)),
        ],
        out_specs=pl.BlockSpec((br, width), lambda i, j: (i, j)),
        out_shape=jax.ShapeDtypeStruct((T, n * width), BF16),
        compiler_params=_params("parallel", "parallel"),
        name="headnorm",
    )(x, gains.reshape(n, 1, dh).astype(F32))


def _conformer_kernel(a_ref, g_ref, ha_ref, hg_ref, w_ref, b_ref, lg_ref, lb_ref, o_ref, buf_ref, ph_ref, y_ref,
                      *, blocks_per_seq):
    bs, C = a_ref.shape
    halo = ha_ref.shape[0]
    kw = w_ref.shape[0]
    first = (pl.program_id(0) % blocks_per_seq) == 0
    buf_ref[halo:halo + bs, :] = a_ref[...].astype(F32) * jax.nn.sigmoid(g_ref[...].astype(F32))
    hc = ha_ref[...].astype(F32) * jax.nn.sigmoid(hg_ref[...].astype(F32))
    buf_ref[0:halo, :] = jnp.where(first, 0.0, hc)
    off = halo - (kw - 1)
    pr = ph_ref.shape[1]
    for b in range(1, 8):
        ph_ref[b - 1] = buf_ref[b:b + pr, :]
    rs, cs = min(bs, 64), min(C, 512)
    for r0 in range(0, bs, rs):
        for c0 in range(0, C, cs):
            acc = jnp.broadcast_to(b_ref[:, c0:c0 + cs], (rs, cs))
            for k in range(kw):
                a8, b = divmod(off + k, 8)
                src = buf_ref if b == 0 else ph_ref.at[b - 1]
                acc = acc + w_ref[k:k + 1, c0:c0 + cs] * src[8 * a8 + r0:8 * a8 + r0 + rs, c0:c0 + cs]
            y_ref[r0:r0 + rs, c0:c0 + cs] = acc
    y = y_ref[...]
    yc = y - jnp.mean(y, axis=-1, keepdims=True)
    z = yc * lax.rsqrt(jnp.mean(yc * yc, axis=-1, keepdims=True) + EPS) * lg_ref[...] + lb_ref[...]
    o_ref[...] = _silu(z).astype(o_ref.dtype)


def conformer_conv(u, conv_w, conv_b, ln_g, ln_b, seq):
    T = u.shape[0]
    kw, C = conv_w.shape
    bs = _tile(seq, CONF_ROWS, CONF_HALO)
    halo = CONF_HALO
    assert kw - 1 <= halo and bs % halo == 0
    hb = bs // halo
    row = lambda v: v.reshape(1, C).astype(F32)
    return pl.pallas_call(
        functools.partial(_conformer_kernel, blocks_per_seq=seq // bs),
        grid=(T // bs,),
        in_specs=[
            pl.BlockSpec((bs, C), lambda i: (i, 0)),
            pl.BlockSpec((bs, C), lambda i: (i, 1)),
            pl.BlockSpec((halo, C), lambda i: (jnp.maximum(i * hb - 1, 0), 0)),
            pl.BlockSpec((halo, C), lambda i: (jnp.maximum(i * hb - 1, 0), 1)),
            pl.BlockSpec((kw, C), lambda i: (0, 0)),
            pl.BlockSpec((1, C), lambda i: (0, 0)),
            pl.BlockSpec((1, C), lambda i: (0, 0)),
            pl.BlockSpec((1, C), lambda i: (0, 0)),
        ],
        out_specs=pl.BlockSpec((bs, C), lambda i: (i, 0)),
        out_shape=jax.ShapeDtypeStruct((T, C), BF16),
        scratch_shapes=[pltpu.VMEM((bs + halo, C), F32), pltpu.VMEM((7, bs + halo - 8, C), F32),
                        pltpu.VMEM((bs, C), F32)],
        compiler_params=_params("parallel"),
        name="conformer_conv",
    )(u, u, u, u, conv_w.astype(F32), row(conv_b), row(ln_g), row(ln_b))


def _qk_prep_kernel(x_ref, g_ref, base_ref, coef_ref, o_ref, *, dh, slopes2, bq):
    br, width = x_ref.shape
    g = g_ref[0]
    period = jnp.where(pl.program_id(1) == 0, bq, 2 * bq)
    pos = ((pl.program_id(0) * br) % period + lax.broadcasted_iota(jnp.int32, (br, 1), 0)).astype(F32)
    lane = lax.broadcasted_iota(jnp.int32, (br, dh), 1)
    for grp in range(width // dh):
        x = x_ref[:, grp * dh:(grp + 1) * dh].astype(F32)
        inv = lax.rsqrt(jnp.mean(x * x, axis=-1, keepdims=True) + EPS)
        o_ref[:, 2 * grp * dh:(2 * grp + 1) * dh] = (x * inv * g).astype(o_ref.dtype)
        hi, mid, lo = (p.astype(F32) for p in _split3(pos * slopes2[grp // 2]))
        pieces = jnp.where(lane % 3 == 0, hi, jnp.where(lane % 3 == 1, mid, lo))
        ext = base_ref[0] + coef_ref[0] * pieces
        o_ref[:, (2 * grp + 1) * dh:(2 * grp + 2) * dh] = ext.astype(o_ref.dtype)


def qk_prep(u, gains, col_block0, width, slopes2, bq, shift):
    T = u.shape[0]
    dh = gains.shape[-1]
    br = _tile(bq, NORM_ROWS, 8)
    lane = jnp.arange(dh)
    sh = jnp.stack([p.astype(F32) for p in _split3(-shift.astype(F32))])
    base_q = jnp.where(lane < 3, 1.0, 0.0).at[6:9].set(sh)
    base_k = jnp.where((lane >= 3) & (lane < 9), 1.0, 0.0)
    coef_q = jnp.where((lane >= 3) & (lane < 6), -1.0, 0.0)
    coef_k = jnp.where(lane < 3, 1.0, 0.0)
    vec = lambda a, b: jnp.stack([a, b]).astype(F32).reshape(2, 1, dh)
    spec = pl.BlockSpec((1, 1, dh), lambda i, j: (j, 0, 0))
    return pl.pallas_call(
        functools.partial(_qk_prep_kernel, dh=dh, slopes2=slopes2, bq=bq),
        grid=(T // br, 2),
        in_specs=[pl.BlockSpec((br, width), lambda i, j: (i, col_block0 + j)), spec, spec, spec],
        out_specs=pl.BlockSpec((br, 2 * width), lambda i, j: (i, j)),
        out_shape=jax.ShapeDtypeStruct((T, 4 * width), BF16),
        compiler_params=_params("parallel", "parallel"),
        name="qk_prep",
    )(u, gains.reshape(2, 1, dh).astype(F32), vec(base_q, base_k), vec(coef_q, coef_k))


def _diff_attn_bounded_kernel(lam_ref, slope_ref, q_ref, k_ref, v_ref, sub_ref, o_ref, acc_ref, l_ref, *, out_scale):
    bq = q_ref.shape[0]
    dk = q_ref.shape[1] // 2
    qi = pl.program_id(2)
    slope2 = slope_ref[0, pl.program_id(1)]
    visible = lax.broadcasted_iota(jnp.int32, (bq, bq), 1) <= lax.broadcasted_iota(jnp.int32, (bq, bq), 0)
    qs = [q_ref[:, c * dk:(c + 1) * dk] for c in range(2)]

    def step(blk, width, masked):
        cj = slope2 * ((blk // 2) * (2 * bq) - qi * bq).astype(F32)
        rows = pl.ds(pl.multiple_of(blk * bq, bq), width)
        vb = v_ref[rows, :]
        for c in range(2):
            s = lax.dot_general(qs[c], k_ref[rows, c * dk:(c + 1) * dk], (((1,), (1,)), ((), ())),
                                preferred_element_type=F32) + cj
            if masked:
                s = jnp.where(visible, s, -jnp.inf)
            p = jnp.exp2(s)
            l_ref[c] += sum(p[:, t:t + LANE] for t in range(0, width, LANE))
            acc_ref[c] += jnp.dot(p.astype(BF16), vb, preferred_element_type=F32)

    acc_ref[...] = jnp.zeros_like(acc_ref)
    l_ref[...] = jnp.zeros_like(l_ref)

    def pair(i, carry):
        step(2 * i, 2 * bq, masked=False)
        return carry

    lax.fori_loop(0, qi // 2, pair, 0)

    @pl.when(qi % 2 == 1)
    def _():
        step(qi - 1, bq, masked=False)

    step(qi, bq, masked=True)
    l0 = jnp.sum(l_ref[0], axis=-1, keepdims=True)
    l1 = jnp.sum(l_ref[1], axis=-1, keepdims=True)
    o = acc_ref[0] / l0 - lam_ref[0, 0] * (acc_ref[1] / l1)
    o = o * lax.rsqrt(jnp.mean(o * o, axis=-1, keepdims=True) + EPS) * sub_ref[...] * out_scale
    o_ref[...] = o.astype(o_ref.dtype)


def _diff_attn_kernel(lam_ref, slope_ref, q_ref, k_ref, v_ref, sub_ref, o_ref, acc_ref, *, out_scale):
    bq = q_ref.shape[0]
    dk = q_ref.shape[1] // 2
    bk = bq
    qi = pl.program_id(2)
    slope2 = slope_ref[0, pl.program_id(1)]
    visible = lax.broadcasted_iota(jnp.int32, (bq, bk), 1) <= lax.broadcasted_iota(jnp.int32, (bq, bk), 0)
    qs = [q_ref[:, c * dk:(c + 1) * dk] for c in range(2)]

    def step(j, carry, masked):
        cj = slope2 * (((j // 2) * 2 - qi) * bk).astype(F32)
        rows = pl.ds(pl.multiple_of(j * bk, bk), bk)
        vb = v_ref[rows, :]
        out = []
        for c in range(2):
            m, l = carry[2 * c], carry[2 * c + 1]
            s = lax.dot_general(qs[c], k_ref[rows, c * dk:(c + 1) * dk], (((1,), (1,)), ((), ())),
                                preferred_element_type=F32)
            if masked:
                s = jnp.where(visible, s, -jnp.inf)
            m_new = jnp.maximum(m, jnp.max(s, axis=-1, keepdims=True) + cj)
            alpha = jnp.exp2(m - m_new)
            p = jnp.exp2(s - (m_new - cj))
            l = alpha * l + jnp.sum(p, axis=-1, keepdims=True)
            acc_ref[c] = alpha * acc_ref[c] + jnp.dot(p.astype(BF16), vb, preferred_element_type=F32)
            out += [m_new, l]
        return tuple(out)

    acc_ref[...] = jnp.zeros_like(acc_ref)
    neg = jnp.full((bq, 1), -jnp.inf, F32)
    zero = jnp.zeros((bq, 1), F32)
    carry = lax.fori_loop(0, qi, functools.partial(step, masked=False), (neg, zero, neg, zero))
    _, l0, _, l1 = step(qi, carry, masked=True)
    o = acc_ref[0] / l0 - lam_ref[0, 0] * (acc_ref[1] / l1)
    o = o * lax.rsqrt(jnp.mean(o * o, axis=-1, keepdims=True) + EPS) * sub_ref[...] * out_scale
    o_ref[...] = o.astype(o_ref.dtype)


def _alibi_slopes2(n_heads):
    return tuple(2.0 ** (-8.0 * (h + 1) / n_heads) * math.log2(math.e) for h in range(n_heads))


def diff_attention(qk, u, v_col_block0, lam, subln, lambda_init, batch, seq, n_heads, bq, bounded):
    T = qk.shape[0]
    dv = subln.shape[-1]
    nq = seq // bq
    slopes2 = jnp.asarray(_alibi_slopes2(n_heads), F32).reshape(1, n_heads)
    body = _diff_attn_bounded_kernel if bounded else _diff_attn_kernel
    scratch = [pltpu.VMEM((2, bq, dv), F32)] + ([pltpu.VMEM((2, bq, LANE), F32)] if bounded else [])
    return pl.pallas_call(
        functools.partial(body, out_scale=1.0 - lambda_init),
        grid=(batch, n_heads, nq),
        in_specs=[
            pl.BlockSpec(memory_space=pltpu.SMEM),
            pl.BlockSpec(memory_space=pltpu.SMEM),
            pl.BlockSpec((bq, 2 * dv), lambda b, h, i: (b * nq + i, h)),
            pl.BlockSpec((seq, 2 * dv), lambda b, h, i: (b, n_heads + h)),
            pl.BlockSpec((seq, dv), lambda b, h, i: (b, v_col_block0 + h)),
            pl.BlockSpec((1, dv), lambda b, h, i: (0, 0)),
        ],
        out_specs=pl.BlockSpec((bq, dv), lambda b, h, i: (b * nq + i, h)),
        out_shape=jax.ShapeDtypeStruct((T, n_heads * dv), BF16),
        scratch_shapes=scratch,
        compiler_params=_params("parallel", "parallel", "arbitrary"),
        name="diff_attention_bounded" if bounded else "diff_attention",
    )(lam.reshape(1, 1).astype(F32), slopes2, qk, qk, u, subln.reshape(1, dv).astype(F32))


def _sconv_kernel(x_ref, hx_ref, w_ref, b_ref, o_ref, buf_ref, *, blocks_per_seq):
    bs, _ = x_ref.shape
    halo = hx_ref.shape[0]
    kw = w_ref.shape[0]
    first = (pl.program_id(1) % blocks_per_seq) == 0
    buf_ref[halo:halo + bs, :] = x_ref[...].astype(F32)
    buf_ref[0:halo, :] = jnp.where(first, 0.0, hx_ref[...].astype(F32))
    xb = buf_ref[...]
    acc = b_ref[...] + w_ref[kw - 1:kw, :] * xb
    for s in range(1, kw):
        acc = acc + w_ref[kw - 1 - s:kw - s, :] * pltpu.roll(xb, s, axis=0)
    o_ref[...] = _silu(acc[halo:, :]).astype(o_ref.dtype)


def ssd_conv(xbc, conv_w, conv_b, seq):
    T, C = xbc.shape
    kw = conv_w.shape[0]
    bs = _tile(seq, SCONV_ROWS, SCONV_HALO)
    halo = SCONV_HALO
    bc = _tile(C, 1024)
    hb = bs // halo
    return pl.pallas_call(
        functools.partial(_sconv_kernel, blocks_per_seq=seq // bs),
        grid=(C // bc, T // bs),
        in_specs=[
            pl.BlockSpec((bs, bc), lambda j, i: (i, j)),
            pl.BlockSpec((halo, bc), lambda j, i: (jnp.maximum(i * hb - 1, 0), j)),
            pl.BlockSpec((kw, bc), lambda j, i: (0, j)),
            pl.BlockSpec((1, bc), lambda j, i: (0, j)),
        ],
        out_specs=pl.BlockSpec((bs, bc), lambda j, i: (i, j)),
        out_shape=jax.ShapeDtypeStruct((T, C), BF16),
        scratch_shapes=[pltpu.VMEM((bs + halo, bc), F32)],
        compiler_params=_params("parallel", "parallel"),
        name="ssd_conv",
    )(xbc, xbc, conv_w.astype(F32), conv_b.reshape(1, C).astype(F32))


def _split3(x):
    hi = x.astype(BF16)
    r = x - hi.astype(F32)
    mid = r.astype(BF16)
    lo = (r - mid.astype(F32)).astype(BF16)
    return hi, mid, lo


def _softplus(x):
    return jnp.maximum(x, 0.0) + jnp.log1p(jnp.exp(-jnp.abs(x)))


def _ssd_kernel(x_ref, z_ref, b_ref, c_ref, dt_ref, dtt_ref, a_ref, at_ref, db_ref, dbt_ref, e_ref, dsk_ref, nw_ref,
                o_ref, state_ref, *, hpg, hd, L):
    hpv = LANE // hd

    @pl.when(pl.program_id(2) == 0)
    def _():
        state_ref[...] = jnp.zeros_like(state_ref)

    ri = lax.broadcasted_iota(jnp.int32, (L, L), 0)
    ci = lax.broadcasted_iota(jnp.int32, (L, L), 1)
    causal = ci <= ri
    tril = causal.astype(BF16)
    triu = (ri <= ci).astype(BF16)
    lane_head = lax.broadcasted_iota(jnp.int32, (L, LANE), 1) // hd

    def expand(v):
        return jnp.dot(jnp.concatenate(_split3(v), axis=1), e_ref[...], preferred_element_type=F32)

    for r0 in range(0, x_ref.shape[0], L):
        rows = slice(r0, r0 + L)
        dt = _softplus(dt_ref[0, 0, rows, :] + db_ref[0])
        dtt = _softplus(dtt_ref[0, 0, :, rows] + dbt_ref[0])
        a = dt * a_ref[0]
        at = dtt * at_ref[0]
        acum = sum(jnp.dot(tril, p, preferred_element_type=F32) for p in _split3(a))
        acumt = sum(jnp.dot(p, triu, preferred_element_type=F32) for p in _split3(at))
        a_last = acum[L - 1:L, :]
        ea_x = expand(jnp.exp(acum))
        wst_x = expand(jnp.exp(a_last - acum) * dt)

        bm = b_ref[rows, :]
        cm = c_ref[rows, :]
        x = x_ref[rows, :]
        x32 = x.astype(F32)
        state = state_ref[...]
        y = jnp.dot(cm, state.astype(BF16), preferred_element_type=F32) * ea_x + dsk_ref[...] * x32
        xw = (x32 * wst_x).astype(BF16)
        upd = lax.dot_general(bm, xw, (((0,), (0,)), ((), ())), preferred_element_type=F32)
        state_ref[...] = state * ea_x[L - 1:L, :] + upd

        cb = lax.dot_general(cm, bm, (((1,), (1,)), ((), ())), preferred_element_type=F32)
        cbm = jnp.where(causal, cb, 0.0)
        groups = []
        for g0 in range(0, hpg, hpv):
            xg = x[:, g0 * hd:(g0 + hpv) * hd]
            yg = None
            for r in range(hpv):
                j = g0 + r
                seg = jnp.minimum(acum[:, j:j + 1] - acumt[j:j + 1, :], 0.0)
                mj = (cbm * jnp.exp(seg) * dtt[j:j + 1, :]).astype(BF16)
                d = jnp.dot(mj, xg, preferred_element_type=F32)
                yg = d if yg is None else jnp.where(lane_head == r, d, yg)
            groups.append(yg)
        y = y + jnp.concatenate(groups, axis=1)

        g = y * _silu(z_ref[rows, :].astype(F32))
        g = g * lax.rsqrt(jnp.mean(g * g, axis=-1, keepdims=True) + EPS) * nw_ref[...]
        o_ref[rows, :] = g.astype(o_ref.dtype)


def ssd_scan(xbc, z, dt_raw, dt_bias, a_log, d_skip, norm_w, batch, seq, d_inner):
    T = xbc.shape[0]
    H = dt_raw.shape[-1]
    N = SSM_STATE
    G = (xbc.shape[1] - d_inner) // (2 * N)
    hpg = H // G
    hd = d_inner // H
    gw = hpg * hd
    L = _tile(seq, SSM_CHUNK, 8)
    lb = _tile(seq, SSM_BLOCK_CHUNKS * L, L)
    nc = seq // lb
    assert gw % LANE == 0 and gw % N == 0 and LANE % hd == 0
    b0 = d_inner // N
    expand = jnp.tile(jnp.repeat(jnp.eye(hpg, dtype=BF16), hd, axis=1), (3, 1))
    dtg = dt_raw.reshape(batch, seq, G, hpg).transpose(0, 2, 1, 3)
    dttg = dtg.transpose(0, 1, 3, 2)
    a = -jnp.exp(a_log.astype(F32)).reshape(G, 1, hpg)
    db = dt_bias.astype(F32).reshape(G, 1, hpg)
    dsk = jnp.repeat(d_skip.astype(F32), hd).reshape(1, d_inner)
    row = lambda b, g, c: (b * nc + c, g)
    return pl.pallas_call(
        functools.partial(_ssd_kernel, hpg=hpg, hd=hd, L=L),
        grid=(batch, G, nc),
        in_specs=[
            pl.BlockSpec((lb, gw), row),
            pl.BlockSpec((lb, gw), row),
            pl.BlockSpec((lb, N), lambda b, g, c: (b * nc + c, b0 + g)),
            pl.BlockSpec((lb, N), lambda b, g, c: (b * nc + c, b0 + G + g)),
            pl.BlockSpec((1, 1, lb, hpg), lambda b, g, c: (b, g, c, 0)),
            pl.BlockSpec((1, 1, hpg, lb), lambda b, g, c: (b, g, 0, c)),
            pl.BlockSpec((1, 1, hpg), lambda b, g, c: (g, 0, 0)),
            pl.BlockSpec((1, hpg, 1), lambda b, g, c: (g, 0, 0)),
            pl.BlockSpec((1, 1, hpg), lambda b, g, c: (g, 0, 0)),
            pl.BlockSpec((1, hpg, 1), lambda b, g, c: (g, 0, 0)),
            pl.BlockSpec((3 * hpg, gw), lambda b, g, c: (0, 0)),
            pl.BlockSpec((1, gw), lambda b, g, c: (0, g)),
            pl.BlockSpec((1, gw), lambda b, g, c: (0, g)),
        ],
        out_specs=pl.BlockSpec((lb, gw), row),
        out_shape=jax.ShapeDtypeStruct((T, d_inner), BF16),
        scratch_shapes=[pltpu.VMEM((N, gw), F32)],
        compiler_params=_params("parallel", "parallel", "arbitrary"),
        name="ssd_scan",
    )(xbc, z, xbc, xbc, dtg, dttg, a, a.reshape(G, hpg, 1), db, db.reshape(G, hpg, 1), expand, dsk,
      norm_w.reshape(1, d_inner).astype(F32))


def _cross_kernel(h_ref, gx_ref, wq_ref, qn_ref, k_ref, v_ref, wo_ref, gf_ref, ho_ref, xn_ref, *, dh):
    h = h_ref[...]
    hn = (h * lax.rsqrt(jnp.mean(h * h, axis=-1, keepdims=True) + EPS) * gx_ref[...]).astype(BF16)
    q = jnp.dot(hn, wq_ref[...], preferred_element_type=F32)
    xd = q.shape[-1]
    heads = []
    for c0 in range(0, xd, dh):
        qh = q[:, c0:c0 + dh]
        qh = qh * lax.rsqrt(jnp.mean(qh * qh, axis=-1, keepdims=True) + EPS) * qn_ref[...]
        s = lax.dot_general(qh.astype(BF16), k_ref[0, :, c0:c0 + dh], (((1,), (1,)), ((), ())),
                            preferred_element_type=F32)
        s = s - jnp.max(s, axis=-1, keepdims=True)
        p = jnp.exp(s)
        p = p / jnp.sum(p, axis=-1, keepdims=True)
        heads.append(jnp.dot(p.astype(BF16), v_ref[0, :, c0:c0 + dh], preferred_element_type=F32))
    o = jnp.concatenate(heads, axis=-1).astype(BF16)
    hnew = h + jnp.dot(o, wo_ref[...], preferred_element_type=F32)
    ho_ref[...] = hnew
    xn = hnew * lax.rsqrt(jnp.mean(hnew * hnew, axis=-1, keepdims=True) + EPS) * gf_ref[...]
    xn_ref[...] = xn.astype(xn_ref.dtype)


def cross_attention(h, g_cross, wq, q_norm, k, v, wo, g_ffn, seq):
    T, D = h.shape
    X = wq.shape[1]
    M = k.shape[1]
    dh = q_norm.shape[-1]
    bm = _tile(seq, CROSS_ROWS, 8)
    per_seq = seq // bm
    const = lambda i: (0, 0)
    resident = dict(pipeline_mode=pl.Buffered(1))
    return pl.pallas_call(
        functools.partial(_cross_kernel, dh=dh),
        grid=(T // bm,),
        in_specs=[
            pl.BlockSpec((bm, D), lambda i: (i, 0)),
            pl.BlockSpec((1, D), const),
            pl.BlockSpec((D, X), const, **resident),
            pl.BlockSpec((1, dh), const),
            pl.BlockSpec((1, M, X), lambda i: (i // per_seq, 0, 0)),
            pl.BlockSpec((1, M, X), lambda i: (i // per_seq, 0, 0)),
            pl.BlockSpec((X, D), const, **resident),
            pl.BlockSpec((1, D), const),
        ],
        out_specs=[pl.BlockSpec((bm, D), lambda i: (i, 0)), pl.BlockSpec((bm, D), lambda i: (i, 0))],
        out_shape=[jax.ShapeDtypeStruct((T, D), F32), jax.ShapeDtypeStruct((T, D), BF16)],
        compiler_params=_params("parallel"),
        name="cross_attention",
    )(h, g_cross.reshape(1, D).astype(F32), wq, q_norm.reshape(1, dh).astype(F32), k, v, wo,
      g_ffn.reshape(1, D).astype(F32))


def _ffn_in_kernel(x_ref, wa_ref, wb_ref, cw_ref, cb_ref, o_ref, buf_ref, *, tiles_per_seq):
    bm = x_ref.shape[0]
    kw = cw_ref.shape[0]
    first = (pl.program_id(1) % tiles_per_seq) == 0
    @pl.when(first)
    def _():
        buf_ref[0:8, :] = jnp.zeros((8, buf_ref.shape[1]), F32)

    @pl.when(jnp.logical_not(first))
    def _():
        buf_ref[0:8, :] = buf_ref[bm:bm + 8, :]

    x = x_ref[...]
    buf_ref[8:8 + bm, :] = jnp.dot(x, wa_ref[...], preferred_element_type=F32)
    b = jnp.dot(x, wb_ref[...], preferred_element_type=F32)
    off = 8 - (kw - 1)
    acc = cb_ref[...] + cw_ref[0:1, :] * buf_ref[off:off + bm, :]
    for k in range(1, kw):
        acc = acc + cw_ref[k:k + 1, :] * buf_ref[off + k:off + k + bm, :]
    o_ref[...] = (_silu(acc) * b).astype(o_ref.dtype)


def ffn_in(xn, w_ab, conv_w, conv_b, seq):
    T, D = xn.shape
    F = w_ab.shape[1] // 2
    bm = _tile(seq, FFN_BM, 8)
    bn = _tile(F, FFN_BN)
    nj = F // bn
    return pl.pallas_call(
        functools.partial(_ffn_in_kernel, tiles_per_seq=seq // bm),
        grid=(nj, T // bm),
        in_specs=[
            pl.BlockSpec((bm, D), lambda j, i: (i, 0)),
            pl.BlockSpec((D, bn), lambda j, i: (0, j)),
            pl.BlockSpec((D, bn), lambda j, i: (0, nj + j)),
            pl.BlockSpec((conv_w.shape[0], bn), lambda j, i: (0, j)),
            pl.BlockSpec((1, bn), lambda j, i: (0, j)),
        ],
        out_specs=pl.BlockSpec((bm, bn), lambda j, i: (i, j)),
        out_shape=jax.ShapeDtypeStruct((T, F), BF16),
        scratch_shapes=[pltpu.VMEM((bm + 8, bn), F32)],
        compiler_params=_params("parallel", "arbitrary"),
        name="ffn_in",
    )(xn, w_ab, w_ab, conv_w, conv_b)


def _pad_cols(w, n):
    return jnp.pad(w, ((0, 0), (0, n - w.shape[1])))


def kernel(x, mem, norm_mix, norm_cross, norm_mem, norm_ffn, xq_w, xk_w, xv_w, xo_w, xq_norm, xk_norm, ffn_in_w, ffn_conv_w, ffn_conv_b, ffn_out_w, ev_in_w, ev_out_w, cv_conv_w, cv_conv_b, cv_ln_g, cv_ln_b, da_q_norm, da_k_norm, da_lq1, da_lk1, da_lq2, da_lk2, da_subln, m_in_w, m_conv_w, m_conv_b, m_dt_bias, m_A_log, m_D, m_norm, m_out_w):
    batch, seq, D = x.shape
    depth = norm_mix.shape[0]
    T = batch * seq
    mem_len = mem.shape[1]
    x_dim = xq_w.shape[-1]
    x_dh = xq_norm.shape[-1]
    d_ff = ffn_conv_w.shape[-1]
    f_pad = -(-d_ff // FFN_PAD) * FFN_PAD
    conv_ch = cv_conv_w.shape[-1]
    da_dh = da_q_norm.shape[-1]
    da_dv = da_subln.shape[-1]
    da_qk = (ev_in_w.shape[-1] - 2 * conv_ch) // 3
    da_heads = da_qk // (2 * da_dh)
    d_inner = m_norm.shape[-1]
    ssm_heads = m_dt_bias.shape[-1]
    assert da_dv == 2 * da_dh and conv_ch % da_qk == 0 and da_qk % da_dv == 0

    h = x.reshape(T, D)
    mem2 = mem.reshape(batch * mem_len, D)

    for i in range(depth):
        xn = rmsnorm(h, norm_mix[i])
        if i % 2 == 0:
            e = i // 2
            lambda_init = 0.8 - 0.6 * math.exp(-0.3 * i)
            u = matmul(xn, cast_cols(ev_in_w, e))
            c = conformer_conv(u, cv_conv_w[e], cv_conv_b[e], cv_ln_g[e], cv_ln_b[e], seq)
            gains = jnp.stack([da_q_norm[e].astype(F32) * (da_dh ** -0.5 * math.log2(math.e)),
                               da_k_norm[e].astype(F32)])
            bq = _tile(seq, ATT_BQ, 8)
            assert seq % (2 * bq) == 0
            shift = 1.02 * da_dh * jnp.max(jnp.abs(gains[0])) * jnp.max(jnp.abs(gains[1])) + 0.1
            qk = qk_prep(u, gains, 2 * conv_ch // da_qk, da_qk, _alibi_slopes2(da_heads), bq, shift)
            lam = (jnp.exp(jnp.sum(da_lq1[e].astype(F32) * da_lk1[e].astype(F32)))
                   - jnp.exp(jnp.sum(da_lq2[e].astype(F32) * da_lk2[e].astype(F32))) + lambda_init)
            attn = functools.partial(diff_attention, qk, u, (2 * conv_ch + 2 * da_qk) // da_dv, lam, da_subln[e],
                                     lambda_init, batch, seq, da_heads, bq)
            o = lax.cond(shift <= ATT_MAX_SHIFT, lambda: attn(True), lambda: attn(False))
            y = jnp.concatenate([c, o], axis=-1)
            h = matmul(y, cast_cols(ev_out_w, e), res=h, out_dtype=F32)
        else:
            o_ = i // 2
            w_in = cast_cols(m_in_w, o_)
            n_xbc = w_in.shape[1] - d_inner - ssm_heads
            z = matmul(xn, w_in, n=d_inner)
            xbc = matmul(xn, w_in, w_col0=d_inner, n=n_xbc)
            dt_raw = matmul(xn, w_in, w_col0=d_inner + n_xbc, n=ssm_heads, out_dtype=F32)
            xbc = ssd_conv(xbc, m_conv_w[o_], m_conv_b[o_], seq)
            g = ssd_scan(xbc, z, dt_raw, m_dt_bias[o_], m_A_log[o_], m_D[o_], m_norm[o_], batch, seq, d_inner)
            h = matmul(g, cast_cols(m_out_w, o_), res=h, out_dtype=F32, bn=MM_WIDE_BN, bk=MM_WIDE_BK)

        memn = rmsnorm(mem2, norm_mem[i])
        kv = matmul(memn, jnp.concatenate([xk_w[i], xv_w[i]], axis=1).astype(BF16), out_dtype=F32)
        kx = headnorm(kv, xk_norm[i].reshape(1, x_dh), 0, x_dim).reshape(batch, mem_len, x_dim)
        vx = kv[:, x_dim:].astype(BF16).reshape(batch, mem_len, x_dim)
        h, xn = cross_attention(h, norm_cross[i], cast_cols(xq_w, i), xq_norm[i].astype(F32) * (x_dh ** -0.5),
                                kx, vx, cast_cols(xo_w, i), norm_ffn[i], seq)

        w_ab = cast_cols(ffn_in_w, i, out_cols=f_pad, halves=2)
        cw = _pad_cols(ffn_conv_w[i].astype(F32), f_pad)
        cb = _pad_cols(ffn_conv_b[i].astype(F32).reshape(1, d_ff), f_pad)
        act = ffn_in(xn, w_ab, cw, cb, seq)
        w_out = cast_rows(ffn_out_w, i, f_pad)
        h = matmul(act, w_out, res=h, out_dtype=F32, bn=MM_WIDE_BN, bk=MM_WIDE_BK)
    return h.reshape(batch, seq, D)
```

```python
import functools
import math

import jax
import jax.numpy as jnp
from jax import lax
from jax.experimental import pallas as pl
from jax.experimental.pallas import tpu as pltpu

F32 = jnp.float32
BF16 = jnp.bfloat16
EPS = 1e-6

V7X_VMEM_LIMIT_BYTES = 56 * 1024 * 1024
LANE = 128

SSM_STATE = 128
SSM_CHUNK = 128
SSM_BLOCK_CHUNKS = 8
FFN_PAD = 1024

MM_BM = 1024
MM_BN = 1024
MM_BK = 4096
MM_WIDE_BN = 2048
MM_WIDE_BK = 1024
NORM_ROWS = 256
CAST_ROWS = 2048
CAST_COLS = 1024
CAST_PAD_ROWS = 256
CONF_ROWS = 128
CONF_HALO = 32
ATT_BQ = 512
ATT_MAX_SHIFT = 48.0
CROSS_ROWS = 256
FFN_BM = 1024
FFN_BN = 512
PCONV_BN = 1024
PCONV_CHUNK = 256


def _tile(dim, pref, align=LANE):
    if dim <= pref:
        return dim
    t = (pref // align) * align
    while t >= align:
        if dim % t == 0:
            return t
        t -= align
    return dim


def _params(*sem):
    return pltpu.CompilerParams(dimension_semantics=sem, vmem_limit_bytes=V7X_VMEM_LIMIT_BYTES)


def _silu(x):
    return x * jax.nn.sigmoid(x)


def _rmsnorm_kernel(x_ref, g_ref, o_ref):
    x = x_ref[...].astype(F32)
    inv = lax.rsqrt(jnp.mean(x * x, axis=-1, keepdims=True) + EPS)
    o_ref[...] = (x * inv * g_ref[...]).astype(o_ref.dtype)


def rmsnorm(x, g, out_dtype=BF16):
    T, D = x.shape
    br = _tile(T, NORM_ROWS, 8)
    return pl.pallas_call(
        _rmsnorm_kernel,
        grid=(T // br,),
        in_specs=[pl.BlockSpec((br, D), lambda i: (i, 0)), pl.BlockSpec((1, D), lambda i: (0, 0))],
        out_specs=pl.BlockSpec((br, D), lambda i: (i, 0)),
        out_shape=jax.ShapeDtypeStruct((T, D), out_dtype),
        compiler_params=_params("parallel"),
        name="rmsnorm",
    )(x, g.reshape(1, D).astype(F32))


def _mm_kernel(*refs, nk, has_res, n_a):
    a_refs, refs = refs[:n_a], refs[n_a:]
    if has_res:
        w_ref, r_ref, o_ref = refs
    else:
        w_ref, o_ref = refs
        r_ref = None

    def run(first, a_ref=a_refs[0]):
        a = a_ref[...]
        for c0 in range(0, o_ref.shape[1], MM_BN):
            cols = slice(c0, min(c0 + MM_BN, o_ref.shape[1]))
            part = jnp.dot(a, w_ref[:, cols], preferred_element_type=F32)
            if not first:
                o_ref[:, cols] += part
            else:
                total = part + r_ref[:, cols] if has_res else part
                o_ref[:, cols] = total.astype(o_ref.dtype)

    if nk == 1:
        run(True)
        return
    k = pl.program_id(2)
    pl.when(k == 0)(functools.partial(run, True))
    if n_a == 1:
        pl.when(k > 0)(functools.partial(run, False))
    else:
        for t in range(1, n_a):
            pl.when(k == t)(functools.partial(run, False, a_refs[t]))


def matmul(a, w, res=None, out_dtype=BF16, bm=MM_BM, bn=MM_BN, bk=MM_BK, w_col0=0, n=None):
    a_list = a if isinstance(a, (tuple, list)) else (a,)
    M = a_list[0].shape[0]
    K = sum(t.shape[1] for t in a_list)
    N = w.shape[1] if n is None else n
    bm, bn = _tile(M, bm, 8), _tile(math.gcd(N, w_col0), bn)
    bk = a_list[0].shape[1] if len(a_list) > 1 else _tile(K, bk)
    nk = K // bk
    assert nk == 1 or out_dtype == F32
    assert w_col0 % bn == 0 and N % bn == 0 and all(t.shape[1] == bk for t in a_list[1:])
    jb = w_col0 // bn
    if len(a_list) > 1:
        in_specs = [pl.BlockSpec((bm, bk), lambda i, j, k: (i, 0)) for _ in a_list]
    else:
        in_specs = [pl.BlockSpec((bm, bk), lambda i, j, k: (i, k))]
    in_specs.append(pl.BlockSpec((bk, bn), lambda i, j, k: (k, jb + j)))
    args = [*a_list, w]
    if res is not None:
        in_specs.append(pl.BlockSpec((bm, bn), lambda i, j, k: (i, j)))
        args.append(res)
    return pl.pallas_call(
        functools.partial(_mm_kernel, nk=nk, has_res=res is not None, n_a=len(a_list)),
        grid=(M // bm, N // bn, nk),
        in_specs=in_specs,
        out_specs=pl.BlockSpec((bm, bn), lambda i, j, k: (i, j)),
        out_shape=jax.ShapeDtypeStruct((M, N), out_dtype),
        compiler_params=_params("parallel", "parallel", "arbitrary"),
        name="matmul",
    )(*args)


def _cast_kernel(x_ref, o_ref, *, n_valid, axis):
    t = pl.program_id(axis)

    @pl.when(t < n_valid)
    def _():
        o_ref[...] = x_ref[...].astype(o_ref.dtype)

    @pl.when(t >= n_valid)
    def _():
        o_ref[...] = jnp.zeros_like(o_ref)


def cast_cols(stack, layer, out_cols=None, halves=1, bc=CAST_COLS):
    _, K, N = stack.shape
    ncols = N // halves
    out_cols = ncols if out_cols is None else out_cols
    bc = _tile(ncols, bc)
    br = _tile(K, CAST_ROWS, 8)
    assert out_cols % bc == 0
    nv, no = ncols // bc, out_cols // bc
    return pl.pallas_call(
        functools.partial(_cast_kernel, n_valid=nv, axis=2),
        grid=(K // br, halves, no),
        in_specs=[pl.BlockSpec((None, br, bc), lambda r, h, c: (layer, r, h * nv + jnp.minimum(c, nv - 1)))],
        out_specs=pl.BlockSpec((br, bc), lambda r, h, c: (r, h * no + c)),
        out_shape=jax.ShapeDtypeStruct((K, halves * out_cols), BF16),
        compiler_params=_params("parallel", "parallel", "parallel"),
        name="cast_cols",
    )(stack)


def cast_rows(stack, layer, out_rows, br=CAST_PAD_ROWS):
    _, K, N = stack.shape
    br = _tile(math.gcd(K, out_rows), br, 8)
    bc = _tile(N, 4096)
    assert out_rows % br == 0 and K % br == 0
    nv = K // br
    return pl.pallas_call(
        functools.partial(_cast_kernel, n_valid=nv, axis=0),
        grid=(out_rows // br, N // bc),
        in_specs=[pl.BlockSpec((None, br, bc), lambda r, c: (layer, jnp.minimum(r, nv - 1), c))],
        out_specs=pl.BlockSpec((br, bc), lambda r, c: (r, c)),
        out_shape=jax.ShapeDtypeStruct((out_rows, N), BF16),
        compiler_params=_params("parallel", "parallel"),
        name="cast_rows",
    )(stack)


def _headnorm_kernel(x_ref, g_ref, o_ref, *, dh):
    width = x_ref.shape[-1]
    g = g_ref[0]
    for c0 in range(0, width, dh):
        x = x_ref[:, c0:c0 + dh].astype(F32)
        inv = lax.rsqrt(jnp.mean(x * x, axis=-1, keepdims=True) + EPS)
        o_ref[:, c0:c0 + dh] = (x * inv * g).astype(o_ref.dtype)


def headnorm(x, gains, col_block0, width):
    T = x.shape[0]
    n, dh = gains.shape
    br = _tile(T, NORM_ROWS, 8)
    return pl.pallas_call(
        functools.partial(_headnorm_kernel, dh=dh),
        grid=(T // br, n),
        in_specs=[
            pl.BlockSpec((br, width), lambda i, j: (i, col_block0 + j)),
            pl.BlockSpec((1, 1, dh), lambda i, j: (j, 0, 0)),
        ],
        out_specs=pl.BlockSpec((br, width), lambda i, j: (i, j)),
        out_shape=jax.ShapeDtypeStruct((T, n * width), BF16),
        compiler_params=_params("parallel", "parallel"),
        name="headnorm",
    )(x, gains.reshape(n, 1, dh).astype(F32))


def _conformer_kernel(a_ref, g_ref, ha_ref, hg_ref, w_ref, b_ref, lg_ref, lb_ref, o_ref, buf_ref, ph_ref, y_ref,
                      *, blocks_per_seq):
    bs, C = a_ref.shape
    halo = ha_ref.shape[0]
    kw = w_ref.shape[0]
    first = (pl.program_id(0) % blocks_per_seq) == 0
    buf_ref[halo:halo + bs, :] = a_ref[...].astype(F32) * jax.nn.sigmoid(g_ref[...].astype(F32))
    hc = ha_ref[...].astype(F32) * jax.nn.sigmoid(hg_ref[...].astype(F32))
    buf_ref[0:halo, :] = jnp.where(first, 0.0, hc)
    off = halo - (kw - 1)
    pr = ph_ref.shape[1]
    for b in range(1, 8):
        ph_ref[b - 1] = buf_ref[b:b + pr, :]
    rs, cs = min(bs, 64), min(C, 512)
    for r0 in range(0, bs, rs):
        for c0 in range(0, C, cs):
            acc = jnp.broadcast_to(b_ref[:, c0:c0 + cs], (rs, cs))
            for k in range(kw):
                a8, b = divmod(off + k, 8)
                src = buf_ref if b == 0 else ph_ref.at[b - 1]
                acc = acc + w_ref[k:k + 1, c0:c0 + cs] * src[8 * a8 + r0:8 * a8 + r0 + rs, c0:c0 + cs]
            y_ref[r0:r0 + rs, c0:c0 + cs] = acc
    y = y_ref[...]
    yc = y - jnp.mean(y, axis=-1, keepdims=True)
    z = yc * lax.rsqrt(jnp.mean(yc * yc, axis=-1, keepdims=True) + EPS) * lg_ref[...] + lb_ref[...]
    o_ref[...] = _silu(z).astype(o_ref.dtype)


def conformer_conv(u, conv_w, conv_b, ln_g, ln_b, seq):
    T = u.shape[0]
    kw, C = conv_w.shape
    bs = _tile(seq, CONF_ROWS, CONF_HALO)
    halo = CONF_HALO
    assert kw - 1 <= halo and bs % halo == 0
    hb = bs // halo
    row = lambda v: v.reshape(1, C).astype(F32)
    return pl.pallas_call(
        functools.partial(_conformer_kernel, blocks_per_seq=seq // bs),
        grid=(T // bs,),
        in_specs=[
            pl.BlockSpec((bs, C), lambda i: (i, 0)),
            pl.BlockSpec((bs, C), lambda i: (i, 1)),
            pl.BlockSpec((halo, C), lambda i: (jnp.maximum(i * hb - 1, 0), 0)),
            pl.BlockSpec((halo, C), lambda i: (jnp.maximum(i * hb - 1, 0), 1)),
            pl.BlockSpec((kw, C), lambda i: (0, 0)),
            pl.BlockSpec((1, C), lambda i: (0, 0)),
            pl.BlockSpec((1, C), lambda i: (0, 0)),
            pl.BlockSpec((1, C), lambda i: (0, 0)),
        ],
        out_specs=pl.BlockSpec((bs, C), lambda i: (i, 0)),
        out_shape=jax.ShapeDtypeStruct((T, C), BF16),
        scratch_shapes=[pltpu.VMEM((bs + halo, C), F32), pltpu.VMEM((7, bs + halo - 8, C), F32),
                        pltpu.VMEM((bs, C), F32)],
        compiler_params=_params("parallel"),
        name="conformer_conv",
    )(u, u, u, u, conv_w.astype(F32), row(conv_b), row(ln_g), row(ln_b))


def _qk_prep_kernel(x_ref, g_ref, base_ref, coef_ref, o_ref, *, dh, slopes2, bq):
    br, width = x_ref.shape
    g = g_ref[0]
    period = jnp.where(pl.program_id(1) == 0, bq, 2 * bq)
    pos = ((pl.program_id(0) * br) % period + lax.broadcasted_iota(jnp.int32, (br, 1), 0)).astype(F32)
    lane = lax.broadcasted_iota(jnp.int32, (br, dh), 1)
    for grp in range(width // dh):
        x = x_ref[:, grp * dh:(grp + 1) * dh].astype(F32)
        inv = lax.rsqrt(jnp.mean(x * x, axis=-1, keepdims=True) + EPS)
        o_ref[:, 2 * grp * dh:(2 * grp + 1) * dh] = (x * inv * g).astype(o_ref.dtype)
        hi, mid, lo = (p.astype(F32) for p in _split3(pos * slopes2[grp // 2]))
        pieces = jnp.where(lane % 3 == 0, hi, jnp.where(lane % 3 == 1, mid, lo))
        ext = base_ref[0] + coef_ref[0] * pieces
        o_ref[:, (2 * grp + 1) * dh:(2 * grp + 2) * dh] = ext.astype(o_ref.dtype)


def qk_prep(u, gains, col_block0, width, slopes2, bq, shift):
    T = u.shape[0]
    dh = gains.shape[-1]
    br = _tile(bq, NORM_ROWS, 8)
    lane = jnp.arange(dh)
    sh = jnp.stack([p.astype(F32) for p in _split3(-shift.astype(F32))])
    base_q = jnp.where(lane < 3, 1.0, 0.0).at[6:9].set(sh)
    base_k = jnp.where((lane >= 3) & (lane < 9), 1.0, 0.0)
    coef_q = jnp.where((lane >= 3) & (lane < 6), -1.0, 0.0)
    coef_k = jnp.where(lane < 3, 1.0, 0.0)
    vec = lambda a, b: jnp.stack([a, b]).astype(F32).reshape(2, 1, dh)
    spec = pl.BlockSpec((1, 1, dh), lambda i, j: (j, 0, 0))
    return pl.pallas_call(
        functools.partial(_qk_prep_kernel, dh=dh, slopes2=slopes2, bq=bq),
        grid=(T // br, 2),
        in_specs=[pl.BlockSpec((br, width), lambda i, j: (i, col_block0 + j)), spec, spec, spec],
        out_specs=pl.BlockSpec((br, 2 * width), lambda i, j: (i, j)),
        out_shape=jax.ShapeDtypeStruct((T, 4 * width), BF16),
        compiler_params=_params("parallel", "parallel"),
        name="qk_prep",
    )(u, gains.reshape(2, 1, dh).astype(F32), vec(base_q, base_k), vec(coef_q, coef_k))


def _diff_attn_bounded_kernel(lam_ref, slope_ref, q_ref, k_ref, v_ref, sub_ref, o_ref, acc_ref, l_ref, *, out_scale):
    bq = q_ref.shape[0]
    dk = q_ref.shape[1] // 2
    qi = pl.program_id(2)
    slope2 = slope_ref[0, pl.program_id(1)]
    visible = lax.broadcasted_iota(jnp.int32, (bq, bq), 1) <= lax.broadcasted_iota(jnp.int32, (bq, bq), 0)
    qs = [q_ref[:, c * dk:(c + 1) * dk] for c in range(2)]

    def step(blk, width, masked):
        cj = slope2 * ((blk // 2) * (2 * bq) - qi * bq).astype(F32)
        rows = pl.ds(pl.multiple_of(blk * bq, bq), width)
        vb = v_ref[rows, :]
        for c in range(2):
            s = lax.dot_general(qs[c], k_ref[rows, c * dk:(c + 1) * dk], (((1,), (1,)), ((), ())),
                                preferred_element_type=F32) + cj
            if masked:
                s = jnp.where(visible, s, -jnp.inf)
            p = jnp.exp2(s)
            l_ref[c] += sum(p[:, t:t + LANE] for t in range(0, width, LANE))
            acc_ref[c] += jnp.dot(p.astype(BF16), vb, preferred_element_type=F32)

    acc_ref[...] = jnp.zeros_like(acc_ref)
    l_ref[...] = jnp.zeros_like(l_ref)

    def pair(i, carry):
        step(2 * i, 2 * bq, masked=False)
        return carry

    lax.fori_loop(0, qi // 2, pair, 0)

    @pl.when(qi % 2 == 1)
    def _():
        step(qi - 1, bq, masked=False)

    step(qi, bq, masked=True)
    l0 = jnp.sum(l_ref[0], axis=-1, keepdims=True)
    l1 = jnp.sum(l_ref[1], axis=-1, keepdims=True)
    o = acc_ref[0] / l0 - lam_ref[0, 0] * (acc_ref[1] / l1)
    o = o * lax.rsqrt(jnp.mean(o * o, axis=-1, keepdims=True) + EPS) * sub_ref[...] * out_scale
    o_ref[...] = o.astype(o_ref.dtype)


def _diff_attn_kernel(lam_ref, slope_ref, q_ref, k_ref, v_ref, sub_ref, o_ref, acc_ref, *, out_scale):
    bq = q_ref.shape[0]
    dk = q_ref.shape[1] // 2
    bk = bq
    qi = pl.program_id(2)
    slope2 = slope_ref[0, pl.program_id(1)]
    visible = lax.broadcasted_iota(jnp.int32, (bq, bk), 1) <= lax.broadcasted_iota(jnp.int32, (bq, bk), 0)
    qs = [q_ref[:, c * dk:(c + 1) * dk] for c in range(2)]

    def step(j, carry, masked):
        cj = slope2 * (((j // 2) * 2 - qi) * bk).astype(F32)
        rows = pl.ds(pl.multiple_of(j * bk, bk), bk)
        vb = v_ref[rows, :]
        out = []
        for c in range(2):
            m, l = carry[2 * c], carry[2 * c + 1]
            s = lax.dot_general(qs[c], k_ref[rows, c * dk:(c + 1) * dk], (((1,), (1,)), ((), ())),
                                preferred_element_type=F32)
            if masked:
                s = jnp.where(visible, s, -jnp.inf)
            m_new = jnp.maximum(m, jnp.max(s, axis=-1, keepdims=True) + cj)
            alpha = jnp.exp2(m - m_new)
            p = jnp.exp2(s - (m_new - cj))
            l = alpha * l + jnp.sum(p, axis=-1, keepdims=True)
            acc_ref[c] = alpha * acc_ref[c] + jnp.dot(p.astype(BF16), vb, preferred_element_type=F32)
            out += [m_new, l]
        return tuple(out)

    acc_ref[...] = jnp.zeros_like(acc_ref)
    neg = jnp.full((bq, 1), -jnp.inf, F32)
    zero = jnp.zeros((bq, 1), F32)
    carry = lax.fori_loop(0, qi, functools.partial(step, masked=False), (neg, zero, neg, zero))
    _, l0, _, l1 = step(qi, carry, masked=True)
    o = acc_ref[0] / l0 - lam_ref[0, 0] * (acc_ref[1] / l1)
    o = o * lax.rsqrt(jnp.mean(o * o, axis=-1, keepdims=True) + EPS) * sub_ref[...] * out_scale
    o_ref[...] = o.astype(o_ref.dtype)


def _alibi_slopes2(n_heads):
    return tuple(2.0 ** (-8.0 * (h + 1) / n_heads) * math.log2(math.e) for h in range(n_heads))


def diff_attention(qk, u, v_col_block0, lam, subln, lambda_init, batch, seq, n_heads, bq, bounded):
    T = qk.shape[0]
    dv = subln.shape[-1]
    nq = seq // bq
    slopes2 = jnp.asarray(_alibi_slopes2(n_heads), F32).reshape(1, n_heads)
    body = _diff_attn_bounded_kernel if bounded else _diff_attn_kernel
    scratch = [pltpu.VMEM((2, bq, dv), F32)] + ([pltpu.VMEM((2, bq, LANE), F32)] if bounded else [])
    return pl.pallas_call(
        functools.partial(body, out_scale=1.0 - lambda_init),
        grid=(batch, n_heads, nq),
        in_specs=[
            pl.BlockSpec(memory_space=pltpu.SMEM),
            pl.BlockSpec(memory_space=pltpu.SMEM),
            pl.BlockSpec((bq, 2 * dv), lambda b, h, i: (b * nq + i, h)),
            pl.BlockSpec((seq, 2 * dv), lambda b, h, i: (b, n_heads + h)),
            pl.BlockSpec((seq, dv), lambda b, h, i: (b, v_col_block0 + h)),
            pl.BlockSpec((1, dv), lambda b, h, i: (0, 0)),
        ],
        out_specs=pl.BlockSpec((bq, dv), lambda b, h, i: (b * nq + i, h)),
        out_shape=jax.ShapeDtypeStruct((T, n_heads * dv), BF16),
        scratch_shapes=scratch,
        compiler_params=_params("parallel", "parallel", "arbitrary"),
        name="diff_attention_bounded" if bounded else "diff_attention",
    )(lam.reshape(1, 1).astype(F32), slopes2, qk, qk, u, subln.reshape(1, dv).astype(F32))


def _proj_conv_kernel(x_ref, w_ref, cw_ref, cb_ref, o_ref, *buf_refs, tiles_per_seq):
    bm = x_ref.shape[0]
    kw = cw_ref.shape[0]
    chunk = buf_refs[0].shape[1]
    first = (pl.program_id(1) % tiles_per_seq) == 0

    @pl.when(first)
    def _():
        for buf_ref in buf_refs:
            buf_ref[0:8, :] = jnp.zeros((8, chunk), F32)

    @pl.when(jnp.logical_not(first))
    def _():
        for buf_ref in buf_refs:
            buf_ref[0:8, :] = buf_ref[bm:bm + 8, :]

    x = x_ref[...]
    off = 8 - (kw - 1)
    for c, buf_ref in enumerate(buf_refs):
        cols = slice(c * chunk, (c + 1) * chunk)
        buf_ref[8:8 + bm, :] = jnp.dot(x, w_ref[:, cols], preferred_element_type=F32)
        acc = cb_ref[:, cols] + cw_ref[0:1, cols] * buf_ref[off:off + bm, :]
        for k in range(1, kw):
            acc = acc + cw_ref[k:k + 1, cols] * buf_ref[off + k:off + k + bm, :]
        o_ref[:, cols] = _silu(acc).astype(o_ref.dtype)


def proj_conv_silu(xn, w, w_col0, n, conv_w, conv_b, seq):
    T, D = xn.shape
    bm = _tile(seq, FFN_BM, 8)
    bn = _tile(math.gcd(n, w_col0), PCONV_BN)
    chunk = _tile(bn, PCONV_CHUNK)
    jb = w_col0 // bn
    assert conv_w.shape[0] - 1 <= 8
    return pl.pallas_call(
        functools.partial(_proj_conv_kernel, tiles_per_seq=seq // bm),
        grid=(n // bn, T // bm),
        in_specs=[
            pl.BlockSpec((bm, D), lambda j, i: (i, 0)),
            pl.BlockSpec((D, bn), lambda j, i: (0, jb + j)),
            pl.BlockSpec((conv_w.shape[0], bn), lambda j, i: (0, j)),
            pl.BlockSpec((1, bn), lambda j, i: (0, j)),
        ],
        out_specs=pl.BlockSpec((bm, bn), lambda j, i: (i, j)),
        out_shape=jax.ShapeDtypeStruct((T, n), BF16),
        scratch_shapes=[pltpu.VMEM((bm + 8, chunk), F32)] * (bn // chunk),
        compiler_params=_params("parallel", "arbitrary"),
        name="proj_conv_silu",
    )(xn, w, conv_w.astype(F32), conv_b.reshape(1, n).astype(F32))


def _split3(x):
    hi = x.astype(BF16)
    r = x - hi.astype(F32)
    mid = r.astype(BF16)
    lo = (r - mid.astype(F32)).astype(BF16)
    return hi, mid, lo


def _softplus(x):
    return jnp.maximum(x, 0.0) + jnp.log1p(jnp.exp(-jnp.abs(x)))


def _ssd_kernel(x_ref, z_ref, b_ref, c_ref, dt_ref, dtt_ref, a_ref, at_ref, db_ref, dbt_ref, e_ref, dsk_ref, nw_ref,
                o_ref, state_ref, *, hpg, hd, L):
    hpv = LANE // hd

    @pl.when(pl.program_id(2) == 0)
    def _():
        state_ref[...] = jnp.zeros_like(state_ref)

    ri = lax.broadcasted_iota(jnp.int32, (L, L), 0)
    ci = lax.broadcasted_iota(jnp.int32, (L, L), 1)
    causal = ci <= ri
    tril = causal.astype(BF16)
    triu = (ri <= ci).astype(BF16)
    lane_head = lax.broadcasted_iota(jnp.int32, (L, LANE), 1) // hd

    def expand(v):
        return jnp.dot(jnp.concatenate(_split3(v), axis=1), e_ref[...], preferred_element_type=F32)

    for r0 in range(0, x_ref.shape[0], L):
        rows = slice(r0, r0 + L)
        dt = _softplus(dt_ref[0, 0, rows, :] + db_ref[0])
        dtt = _softplus(dtt_ref[0, 0, :, rows] + dbt_ref[0])
        a = dt * a_ref[0]
        at = dtt * at_ref[0]
        acum = sum(jnp.dot(tril, p, preferred_element_type=F32) for p in _split3(a))
        acumt = sum(jnp.dot(p, triu, preferred_element_type=F32) for p in _split3(at))
        a_last = acum[L - 1:L, :]
        ea_x = expand(jnp.exp(acum))
        wst_x = expand(jnp.exp(a_last - acum) * dt)

        bm = b_ref[rows, :]
        cm = c_ref[rows, :]
        x = x_ref[rows, :]
        x32 = x.astype(F32)
        state = state_ref[...]
        y = jnp.dot(cm, state.astype(BF16), preferred_element_type=F32) * ea_x + dsk_ref[...] * x32
        xw = (x32 * wst_x).astype(BF16)
        upd = lax.dot_general(bm, xw, (((0,), (0,)), ((), ())), preferred_element_type=F32)
        state_ref[...] = state * ea_x[L - 1:L, :] + upd

        cb = lax.dot_general(cm, bm, (((1,), (1,)), ((), ())), preferred_element_type=F32)
        cbm = jnp.where(causal, cb, 0.0)
        groups = []
        for g0 in range(0, hpg, hpv):
            xg = x[:, g0 * hd:(g0 + hpv) * hd]
            yg = None
            for r in range(hpv):
                j = g0 + r
                seg = jnp.minimum(acum[:, j:j + 1] - acumt[j:j + 1, :], 0.0)
                mj = (cbm * jnp.exp(seg) * dtt[j:j + 1, :]).astype(BF16)
                d = jnp.dot(mj, xg, preferred_element_type=F32)
                yg = d if yg is None else jnp.where(lane_head == r, d, yg)
            groups.append(yg)
        y = y + jnp.concatenate(groups, axis=1)

        g = y * _silu(z_ref[rows, :].astype(F32))
        g = g * lax.rsqrt(jnp.mean(g * g, axis=-1, keepdims=True) + EPS) * nw_ref[...]
        o_ref[rows, :] = g.astype(o_ref.dtype)


def ssd_scan(xbc, z, dt_raw, dt_bias, a_log, d_skip, norm_w, batch, seq, d_inner):
    T = xbc.shape[0]
    H = dt_raw.shape[-1]
    N = SSM_STATE
    G = (xbc.shape[1] - d_inner) // (2 * N)
    hpg = H // G
    hd = d_inner // H
    gw = hpg * hd
    L = _tile(seq, SSM_CHUNK, 8)
    lb = _tile(seq, SSM_BLOCK_CHUNKS * L, L)
    nc = seq // lb
    assert gw % LANE == 0 and gw % N == 0 and LANE % hd == 0
    b0 = d_inner // N
    expand = jnp.tile(jnp.repeat(jnp.eye(hpg, dtype=BF16), hd, axis=1), (3, 1))
    dtg = dt_raw.reshape(batch, seq, G, hpg).transpose(0, 2, 1, 3)
    dttg = dtg.transpose(0, 1, 3, 2)
    a = -jnp.exp(a_log.astype(F32)).reshape(G, 1, hpg)
    db = dt_bias.astype(F32).reshape(G, 1, hpg)
    dsk = jnp.repeat(d_skip.astype(F32), hd).reshape(1, d_inner)
    row = lambda b, g, c: (b * nc + c, g)
    return pl.pallas_call(
        functools.partial(_ssd_kernel, hpg=hpg, hd=hd, L=L),
        grid=(batch, G, nc),
        in_specs=[
            pl.BlockSpec((lb, gw), row),
            pl.BlockSpec((lb, gw), row),
            pl.BlockSpec((lb, N), lambda b, g, c: (b * nc + c, b0 + g)),
            pl.BlockSpec((lb, N), lambda b, g, c: (b * nc + c, b0 + G + g)),
            pl.BlockSpec((1, 1, lb, hpg), lambda b, g, c: (b, g, c, 0)),
            pl.BlockSpec((1, 1, hpg, lb), lambda b, g, c: (b, g, 0, c)),
            pl.BlockSpec((1, 1, hpg), lambda b, g, c: (g, 0, 0)),
            pl.BlockSpec((1, hpg, 1), lambda b, g, c: (g, 0, 0)),
            pl.BlockSpec((1, 1, hpg), lambda b, g, c: (g, 0, 0)),
            pl.BlockSpec((1, hpg, 1), lambda b, g, c: (g, 0, 0)),
            pl.BlockSpec((3 * hpg, gw), lambda b, g, c: (0, 0)),
            pl.BlockSpec((1, gw), lambda b, g, c: (0, g)),
            pl.BlockSpec((1, gw), lambda b, g, c: (0, g)),
        ],
        out_specs=pl.BlockSpec((lb, gw), row),
        out_shape=jax.ShapeDtypeStruct((T, d_inner), BF16),
        scratch_shapes=[pltpu.VMEM((N, gw), F32)],
        compiler_params=_params("parallel", "parallel", "arbitrary"),
        name="ssd_scan",
    )(xbc, z, xbc, xbc, dtg, dttg, a, a.reshape(G, hpg, 1), db, db.reshape(G, hpg, 1), expand, dsk,
      norm_w.reshape(1, d_inner).astype(F32))


def _cross_kernel(h_ref, gx_ref, wq_ref, qn_ref, k_ref, v_ref, wo_ref, gf_ref, ho_ref, xn_ref, *, dh):
    h = h_ref[...]
    hn = (h * lax.rsqrt(jnp.mean(h * h, axis=-1, keepdims=True) + EPS) * gx_ref[...]).astype(BF16)
    q = jnp.dot(hn, wq_ref[...], preferred_element_type=F32)
    xd = q.shape[-1]
    heads = []
    for c0 in range(0, xd, dh):
        qh = q[:, c0:c0 + dh]
        qh = qh * lax.rsqrt(jnp.mean(qh * qh, axis=-1, keepdims=True) + EPS) * qn_ref[...]
        s = lax.dot_general(qh.astype(BF16), k_ref[0, :, c0:c0 + dh], (((1,), (1,)), ((), ())),
                            preferred_element_type=F32)
        s = s - jnp.max(s, axis=-1, keepdims=True)
        p = jnp.exp(s)
        p = p / jnp.sum(p, axis=-1, keepdims=True)
        heads.append(jnp.dot(p.astype(BF16), v_ref[0, :, c0:c0 + dh], preferred_element_type=F32))
    o = jnp.concatenate(heads, axis=-1).astype(BF16)
    hnew = h + jnp.dot(o, wo_ref[...], preferred_element_type=F32)
    ho_ref[...] = hnew
    xn = hnew * lax.rsqrt(jnp.mean(hnew * hnew, axis=-1, keepdims=True) + EPS) * gf_ref[...]
    xn_ref[...] = xn.astype(xn_ref.dtype)


def cross_attention(h, g_cross, wq, q_norm, k, v, wo, g_ffn, seq):
    T, D = h.shape
    X = wq.shape[1]
    M = k.shape[1]
    dh = q_norm.shape[-1]
    bm = _tile(seq, CROSS_ROWS, 8)
    per_seq = seq // bm
    const = lambda i: (0, 0)
    resident = dict(pipeline_mode=pl.Buffered(1))
    return pl.pallas_call(
        functools.partial(_cross_kernel, dh=dh),
        grid=(T // bm,),
        in_specs=[
            pl.BlockSpec((bm, D), lambda i: (i, 0)),
            pl.BlockSpec((1, D), const),
            pl.BlockSpec((D, X), const, **resident),
            pl.BlockSpec((1, dh), const),
            pl.BlockSpec((1, M, X), lambda i: (i // per_seq, 0, 0)),
            pl.BlockSpec((1, M, X), lambda i: (i // per_seq, 0, 0)),
            pl.BlockSpec((X, D), const, **resident),
            pl.BlockSpec((1, D), const),
        ],
        out_specs=[pl.BlockSpec((bm, D), lambda i: (i, 0)), pl.BlockSpec((bm, D), lambda i: (i, 0))],
        out_shape=[jax.ShapeDtypeStruct((T, D), F32), jax.ShapeDtypeStruct((T, D), BF16)],
        compiler_params=_params("parallel"),
        name="cross_attention",
    )(h, g_cross.reshape(1, D).astype(F32), wq, q_norm.reshape(1, dh).astype(F32), k, v, wo,
      g_ffn.reshape(1, D).astype(F32))


def _ffn_in_kernel(x_ref, wa_ref, wb_ref, cw_ref, cb_ref, o_ref, buf_ref, *, tiles_per_seq):
    bm = x_ref.shape[0]
    kw = cw_ref.shape[0]
    first = (pl.program_id(1) % tiles_per_seq) == 0
    @pl.when(first)
    def _():
        buf_ref[0:8, :] = jnp.zeros((8, buf_ref.shape[1]), F32)

    @pl.when(jnp.logical_not(first))
    def _():
        buf_ref[0:8, :] = buf_ref[bm:bm + 8, :]

    x = x_ref[...]
    buf_ref[8:8 + bm, :] = jnp.dot(x, wa_ref[...], preferred_element_type=F32)
    b = jnp.dot(x, wb_ref[...], preferred_element_type=F32)
    off = 8 - (kw - 1)
    acc = cb_ref[...] + cw_ref[0:1, :] * buf_ref[off:off + bm, :]
    for k in range(1, kw):
        acc = acc + cw_ref[k:k + 1, :] * buf_ref[off + k:off + k + bm, :]
    o_ref[...] = (_silu(acc) * b).astype(o_ref.dtype)


def ffn_in(xn, w_ab, conv_w, conv_b, seq):
    T, D = xn.shape
    F = w_ab.shape[1] // 2
    bm = _tile(seq, FFN_BM, 8)
    bn = _tile(F, FFN_BN)
    nj = F // bn
    return pl.pallas_call(
        functools.partial(_ffn_in_kernel, tiles_per_seq=seq // bm),
        grid=(nj, T // bm),
        in_specs=[
            pl.BlockSpec((bm, D), lambda j, i: (i, 0)),
            pl.BlockSpec((D, bn), lambda j, i: (0, j)),
            pl.BlockSpec((D, bn), lambda j, i: (0, nj + j)),
            pl.BlockSpec((conv_w.shape[0], bn), lambda j, i: (0, j)),
            pl.BlockSpec((1, bn), lambda j, i: (0, j)),
        ],
        out_specs=pl.BlockSpec((bm, bn), lambda j, i: (i, j)),
        out_shape=jax.ShapeDtypeStruct((T, F), BF16),
        scratch_shapes=[pltpu.VMEM((bm + 8, bn), F32)],
        compiler_params=_params("parallel", "arbitrary"),
        name="ffn_in",
    )(xn, w_ab, w_ab, conv_w, conv_b)


def _pad_cols(w, n):
    return jnp.pad(w, ((0, 0), (0, n - w.shape[1])))


def kernel(x, mem, norm_mix, norm_cross, norm_mem, norm_ffn, xq_w, xk_w, xv_w, xo_w, xq_norm, xk_norm, ffn_in_w, ffn_conv_w, ffn_conv_b, ffn_out_w, ev_in_w, ev_out_w, cv_conv_w, cv_conv_b, cv_ln_g, cv_ln_b, da_q_norm, da_k_norm, da_lq1, da_lk1, da_lq2, da_lk2, da_subln, m_in_w, m_conv_w, m_conv_b, m_dt_bias, m_A_log, m_D, m_norm, m_out_w):
    batch, seq, D = x.shape
    depth = norm_mix.shape[0]
    T = batch * seq
    mem_len = mem.shape[1]
    x_dim = xq_w.shape[-1]
    x_dh = xq_norm.shape[-1]
    d_ff = ffn_conv_w.shape[-1]
    f_pad = -(-d_ff // FFN_PAD) * FFN_PAD
    conv_ch = cv_conv_w.shape[-1]
    da_dh = da_q_norm.shape[-1]
    da_dv = da_subln.shape[-1]
    da_qk = (ev_in_w.shape[-1] - 2 * conv_ch) // 3
    da_heads = da_qk // (2 * da_dh)
    d_inner = m_norm.shape[-1]
    ssm_heads = m_dt_bias.shape[-1]
    assert da_dv == 2 * da_dh and conv_ch % da_qk == 0 and da_qk % da_dv == 0

    h = x.reshape(T, D)
    mem2 = mem.reshape(batch * mem_len, D)

    for i in range(depth):
        xn = rmsnorm(h, norm_mix[i])
        if i % 2 == 0:
            e = i // 2
            lambda_init = 0.8 - 0.6 * math.exp(-0.3 * i)
            u = matmul(xn, cast_cols(ev_in_w, e))
            c = conformer_conv(u, cv_conv_w[e], cv_conv_b[e], cv_ln_g[e], cv_ln_b[e], seq)
            gains = jnp.stack([da_q_norm[e].astype(F32) * (da_dh ** -0.5 * math.log2(math.e)),
                               da_k_norm[e].astype(F32)])
            bq = _tile(seq, ATT_BQ, 8)
            assert seq % (2 * bq) == 0
            shift = 1.02 * da_dh * jnp.max(jnp.abs(gains[0])) * jnp.max(jnp.abs(gains[1])) + 0.1
            qk = qk_prep(u, gains, 2 * conv_ch // da_qk, da_qk, _alibi_slopes2(da_heads), bq, shift)
            lam = (jnp.exp(jnp.sum(da_lq1[e].astype(F32) * da_lk1[e].astype(F32)))
                   - jnp.exp(jnp.sum(da_lq2[e].astype(F32) * da_lk2[e].astype(F32))) + lambda_init)
            attn = functools.partial(diff_attention, qk, u, (2 * conv_ch + 2 * da_qk) // da_dv, lam, da_subln[e],
                                     lambda_init, batch, seq, da_heads, bq)
            o = lax.cond(shift <= ATT_MAX_SHIFT, lambda: attn(True), lambda: attn(False))
            h = matmul((c, o), cast_cols(ev_out_w, e), res=h, out_dtype=F32)
        else:
            o_ = i // 2
            w_in = cast_cols(m_in_w, o_)
            n_xbc = w_in.shape[1] - d_inner - ssm_heads
            z = matmul(xn, w_in, n=d_inner)
            xbc = proj_conv_silu(xn, w_in, d_inner, n_xbc, m_conv_w[o_], m_conv_b[o_], seq)
            dt_raw = matmul(xn, w_in, w_col0=d_inner + n_xbc, n=ssm_heads, out_dtype=F32)
            g = ssd_scan(xbc, z, dt_raw, m_dt_bias[o_], m_A_log[o_], m_D[o_], m_norm[o_], batch, seq, d_inner)
            h = matmul(g, cast_cols(m_out_w, o_), res=h, out_dtype=F32, bn=MM_WIDE_BN, bk=MM_WIDE_BK)

        memn = rmsnorm(mem2, norm_mem[i])
        kv = matmul(memn, jnp.concatenate([xk_w[i], xv_w[i]], axis=1).astype(BF16), out_dtype=F32)
        kx = headnorm(kv, xk_norm[i].reshape(1, x_dh), 0, x_dim).reshape(batch, mem_len, x_dim)
        vx = kv[:, x_dim:].astype(BF16).reshape(batch, mem_len, x_dim)
        h, xn = cross_attention(h, norm_cross[i], cast_cols(xq_w, i), xq_norm[i].astype(F32) * (x_dh ** -0.5),
                                kx, vx, cast_cols(xo_w, i), norm_ffn[i], seq)

        w_ab = cast_cols(ffn_in_w, i, out_cols=f_pad, halves=2)
        cw = _pad_cols(ffn_conv_w[i].astype(F32), f_pad)
        cb = _pad_cols(ffn_conv_b[i].astype(F32).reshape(1, d_ff), f_pad)
        act = ffn_in(xn, w_ab, cw, cb, seq)
        w_out = cast_rows(ffn_out_w, i, f_pad)
        h = matmul(act, w_out, res=h, out_dtype=F32, bn=MM_WIDE_BN, bk=MM_WIDE_BK)
    return h.reshape(batch, seq, D)
```

```python
import functools
import math

import jax
import jax.numpy as jnp
from jax import lax
from jax.experimental import pallas as pl
from jax.experimental.pallas import tpu as pltpu

F32 = jnp.float32
BF16 = jnp.bfloat16
EPS = 1e-6

V7X_VMEM_LIMIT_BYTES = 56 * 1024 * 1024
LANE = 128

SSM_STATE = 128
SSM_CHUNK = 128
SSM_BLOCK_CHUNKS = 8
FFN_PAD = 1024

MM_BM = 1024
MM_BN = 1024
MM_BK = 4096
MM_WIDE_BN = 2048
MM_WIDE_BK = 1024
NORM_ROWS = 256
CAST_ROWS = 4096
CAST_COLS = 1024
CAST_PAD_ROWS = 256
CONF_ROWS = 128
CONF_HALO = 32
ATT_BQ = 512
ATT_MAX_SHIFT = 48.0
CROSS_ROWS = 256
FFN_BM = 1024
FFN_BN = 512
PCONV_BN = 1024
PCONV_CHUNK = 256


def _tile(dim, pref, align=LANE):
    if dim <= pref:
        return dim
    t = (pref // align) * align
    while t >= align:
        if dim % t == 0:
            return t
        t -= align
    return dim


def _params(*sem):
    return pltpu.CompilerParams(dimension_semantics=sem, vmem_limit_bytes=V7X_VMEM_LIMIT_BYTES)


def _silu(x):
    return x * jax.nn.sigmoid(x)


def _rmsnorm_kernel(x_ref, g_ref, o_ref):
    x = x_ref[...].astype(F32)
    inv = lax.rsqrt(jnp.mean(x * x, axis=-1, keepdims=True) + EPS)
    o_ref[...] = (x * inv * g_ref[...]).astype(o_ref.dtype)


def rmsnorm(x, g, out_dtype=BF16):
    T, D = x.shape
    br = _tile(T, NORM_ROWS, 8)
    return pl.pallas_call(
        _rmsnorm_kernel,
        grid=(T // br,),
        in_specs=[pl.BlockSpec((br, D), lambda i: (i, 0)), pl.BlockSpec((1, D), lambda i: (0, 0))],
        out_specs=pl.BlockSpec((br, D), lambda i: (i, 0)),
        out_shape=jax.ShapeDtypeStruct((T, D), out_dtype),
        compiler_params=_params("parallel"),
        name="rmsnorm",
    )(x, g.reshape(1, D).astype(F32))


def _mm_kernel(*refs, nk, has_res, n_a):
    a_refs, refs = refs[:n_a], refs[n_a:]
    if has_res:
        w_ref, r_ref, o_ref = refs
    else:
        w_ref, o_ref = refs
        r_ref = None

    def run(first, a_ref=a_refs[0]):
        a = a_ref[...]
        for c0 in range(0, o_ref.shape[1], MM_BN):
            cols = slice(c0, min(c0 + MM_BN, o_ref.shape[1]))
            part = jnp.dot(a, w_ref[:, cols], preferred_element_type=F32)
            if not first:
                o_ref[:, cols] += part
            else:
                total = part + r_ref[:, cols] if has_res else part
                o_ref[:, cols] = total.astype(o_ref.dtype)

    if nk == 1:
        run(True)
        return
    k = pl.program_id(2)
    pl.when(k == 0)(functools.partial(run, True))
    if n_a == 1:
        pl.when(k > 0)(functools.partial(run, False))
    else:
        for t in range(1, n_a):
            pl.when(k == t)(functools.partial(run, False, a_refs[t]))


def matmul(a, w, res=None, out_dtype=BF16, bm=MM_BM, bn=MM_BN, bk=MM_BK, w_col0=0, n=None):
    a_list = a if isinstance(a, (tuple, list)) else (a,)
    M = a_list[0].shape[0]
    K = sum(t.shape[1] for t in a_list)
    N = w.shape[1] if n is None else n
    bm, bn = _tile(M, bm, 8), _tile(math.gcd(N, w_col0), bn)
    bk = a_list[0].shape[1] if len(a_list) > 1 else _tile(K, bk)
    nk = K // bk
    assert nk == 1 or out_dtype == F32
    assert w_col0 % bn == 0 and N % bn == 0 and all(t.shape[1] == bk for t in a_list[1:])
    jb = w_col0 // bn
    if len(a_list) > 1:
        in_specs = [pl.BlockSpec((bm, bk), lambda i, j, k: (i, 0)) for _ in a_list]
    else:
        in_specs = [pl.BlockSpec((bm, bk), lambda i, j, k: (i, k))]
    in_specs.append(pl.BlockSpec((bk, bn), lambda i, j, k: (k, jb + j)))
    args = [*a_list, w]
    if res is not None:
        in_specs.append(pl.BlockSpec((bm, bn), lambda i, j, k: (i, j)))
        args.append(res)
    return pl.pallas_call(
        functools.partial(_mm_kernel, nk=nk, has_res=res is not None, n_a=len(a_list)),
        grid=(M // bm, N // bn, nk),
        in_specs=in_specs,
        out_specs=pl.BlockSpec((bm, bn), lambda i, j, k: (i, j)),
        out_shape=jax.ShapeDtypeStruct((M, N), out_dtype),
        compiler_params=_params("parallel", "parallel", "arbitrary"),
        name="matmul",
    )(*args)


def _cast_kernel(x_ref, o_ref, *, n_valid, axis):
    t = pl.program_id(axis)

    @pl.when(t < n_valid)
    def _():
        o_ref[...] = x_ref[...].astype(o_ref.dtype)

    @pl.when(t >= n_valid)
    def _():
        o_ref[...] = jnp.zeros_like(o_ref)


def cast_cols(stack, layer, out_cols=None, halves=1, bc=CAST_COLS):
    _, K, N = stack.shape
    ncols = N // halves
    out_cols = ncols if out_cols is None else out_cols
    bc = _tile(ncols, bc)
    br = _tile(K, CAST_ROWS, 8)
    assert out_cols % bc == 0
    nv, no = ncols // bc, out_cols // bc
    return pl.pallas_call(
        functools.partial(_cast_kernel, n_valid=nv, axis=2),
        grid=(K // br, halves, no),
        in_specs=[pl.BlockSpec((None, br, bc), lambda r, h, c: (layer, r, h * nv + jnp.minimum(c, nv - 1)))],
        out_specs=pl.BlockSpec((br, bc), lambda r, h, c: (r, h * no + c)),
        out_shape=jax.ShapeDtypeStruct((K, halves * out_cols), BF16),
        compiler_params=_params("parallel", "parallel", "parallel"),
        name="cast_cols",
    )(stack)


def cast_rows(stack, layer, out_rows, br=CAST_PAD_ROWS):
    _, K, N = stack.shape
    br = _tile(math.gcd(K, out_rows), br, 8)
    bc = _tile(N, 4096)
    assert out_rows % br == 0 and K % br == 0
    nv = K // br
    return pl.pallas_call(
        functools.partial(_cast_kernel, n_valid=nv, axis=0),
        grid=(out_rows // br, N // bc),
        in_specs=[pl.BlockSpec((None, br, bc), lambda r, c: (layer, jnp.minimum(r, nv - 1), c))],
        out_specs=pl.BlockSpec((br, bc), lambda r, c: (r, c)),
        out_shape=jax.ShapeDtypeStruct((out_rows, N), BF16),
        compiler_params=_params("parallel", "parallel"),
        name="cast_rows",
    )(stack)


def _headnorm_kernel(x_ref, g_ref, o_ref, *, dh):
    width = x_ref.shape[-1]
    g = g_ref[0]
    for c0 in range(0, width, dh):
        x = x_ref[:, c0:c0 + dh].astype(F32)
        inv = lax.rsqrt(jnp.mean(x * x, axis=-1, keepdims=True) + EPS)
        o_ref[:, c0:c0 + dh] = (x * inv * g).astype(o_ref.dtype)


def headnorm(x, gains, col_block0, width):
    T = x.shape[0]
    n, dh = gains.shape
    br = _tile(T, NORM_ROWS, 8)
    return pl.pallas_call(
        functools.partial(_headnorm_kernel, dh=dh),
        grid=(T // br, n),
        in_specs=[
            pl.BlockSpec((br, width), lambda i, j: (i, col_block0 + j)),
            pl.BlockSpec((1, 1, dh), lambda i, j: (j, 0, 0)),
        ],
        out_specs=pl.BlockSpec((br, width), lambda i, j: (i, j)),
        out_shape=jax.ShapeDtypeStruct((T, n * width), BF16),
        compiler_params=_params("parallel", "parallel"),
        name="headnorm",
    )(x, gains.reshape(n, 1, dh).astype(F32))


def _conformer_kernel(a_ref, g_ref, ha_ref, hg_ref, w_ref, b_ref, lg_ref, lb_ref, o_ref, buf_ref, ph_ref, y_ref,
                      *, blocks_per_seq):
    bs, C = a_ref.shape
    halo = ha_ref.shape[0]
    kw = w_ref.shape[0]
    first = (pl.program_id(0) % blocks_per_seq) == 0
    buf_ref[halo:halo + bs, :] = a_ref[...].astype(F32) * jax.nn.sigmoid(g_ref[...].astype(F32))
    hc = ha_ref[...].astype(F32) * jax.nn.sigmoid(hg_ref[...].astype(F32))
    buf_ref[0:halo, :] = jnp.where(first, 0.0, hc)
    off = halo - (kw - 1)
    pr = ph_ref.shape[1]
    for b in range(1, 8):
        ph_ref[b - 1] = buf_ref[b:b + pr, :]
    rs, cs = min(bs, 64), min(C, 512)
    for r0 in range(0, bs, rs):
        for c0 in range(0, C, cs):
            acc = jnp.broadcast_to(b_ref[:, c0:c0 + cs], (rs, cs))
            for k in range(kw):
                a8, b = divmod(off + k, 8)
                src = buf_ref if b == 0 else ph_ref.at[b - 1]
                acc = acc + w_ref[k:k + 1, c0:c0 + cs] * src[8 * a8 + r0:8 * a8 + r0 + rs, c0:c0 + cs]
            y_ref[r0:r0 + rs, c0:c0 + cs] = acc
    y = y_ref[...]
    yc = y - jnp.mean(y, axis=-1, keepdims=True)
    z = yc * lax.rsqrt(jnp.mean(yc * yc, axis=-1, keepdims=True) + EPS) * lg_ref[...] + lb_ref[...]
    o_ref[...] = _silu(z).astype(o_ref.dtype)


def conformer_conv(u, conv_w, conv_b, ln_g, ln_b, seq):
    T = u.shape[0]
    kw, C = conv_w.shape
    bs = _tile(seq, CONF_ROWS, CONF_HALO)
    halo = CONF_HALO
    assert kw - 1 <= halo and bs % halo == 0
    hb = bs // halo
    row = lambda v: v.reshape(1, C).astype(F32)
    return pl.pallas_call(
        functools.partial(_conformer_kernel, blocks_per_seq=seq // bs),
        grid=(T // bs,),
        in_specs=[
            pl.BlockSpec((bs, C), lambda i: (i, 0)),
            pl.BlockSpec((bs, C), lambda i: (i, 1)),
            pl.BlockSpec((halo, C), lambda i: (jnp.maximum(i * hb - 1, 0), 0)),
            pl.BlockSpec((halo, C), lambda i: (jnp.maximum(i * hb - 1, 0), 1)),
            pl.BlockSpec((kw, C), lambda i: (0, 0)),
            pl.BlockSpec((1, C), lambda i: (0, 0)),
            pl.BlockSpec((1, C), lambda i: (0, 0)),
            pl.BlockSpec((1, C), lambda i: (0, 0)),
        ],
        out_specs=pl.BlockSpec((bs, C), lambda i: (i, 0)),
        out_shape=jax.ShapeDtypeStruct((T, C), BF16),
        scratch_shapes=[pltpu.VMEM((bs + halo, C), F32), pltpu.VMEM((7, bs + halo - 8, C), F32),
                        pltpu.VMEM((bs, C), F32)],
        compiler_params=_params("parallel"),
        name="conformer_conv",
    )(u, u, u, u, conv_w.astype(F32), row(conv_b), row(ln_g), row(ln_b))


def _qk_prep_kernel(x_ref, g_ref, base_ref, coef_ref, o_ref, *, dh, slopes2, bq):
    br, width = x_ref.shape
    g = g_ref[0]
    period = jnp.where(pl.program_id(1) == 0, bq, 2 * bq)
    pos = ((pl.program_id(0) * br) % period + lax.broadcasted_iota(jnp.int32, (br, 1), 0)).astype(F32)
    lane = lax.broadcasted_iota(jnp.int32, (br, dh), 1)
    for grp in range(width // dh):
        x = x_ref[:, grp * dh:(grp + 1) * dh].astype(F32)
        inv = lax.rsqrt(jnp.mean(x * x, axis=-1, keepdims=True) + EPS)
        o_ref[:, 2 * grp * dh:(2 * grp + 1) * dh] = (x * inv * g).astype(o_ref.dtype)
        hi, mid, lo = (p.astype(F32) for p in _split3(pos * slopes2[grp // 2]))
        pieces = jnp.where(lane % 3 == 0, hi, jnp.where(lane % 3 == 1, mid, lo))
        ext = base_ref[0] + coef_ref[0] * pieces
        o_ref[:, (2 * grp + 1) * dh:(2 * grp + 2) * dh] = ext.astype(o_ref.dtype)


def qk_prep(u, gains, col_block0, width, slopes2, bq, shift):
    T = u.shape[0]
    dh = gains.shape[-1]
    br = _tile(bq, NORM_ROWS, 8)
    lane = jnp.arange(dh)
    sh = jnp.stack([p.astype(F32) for p in _split3(-shift.astype(F32))])
    base_q = jnp.where(lane < 3, 1.0, 0.0).at[6:9].set(sh)
    base_k = jnp.where((lane >= 3) & (lane < 9), 1.0, 0.0)
    coef_q = jnp.where((lane >= 3) & (lane < 6), -1.0, 0.0)
    coef_k = jnp.where(lane < 3, 1.0, 0.0)
    vec = lambda a, b: jnp.stack([a, b]).astype(F32).reshape(2, 1, dh)
    spec = pl.BlockSpec((1, 1, dh), lambda i, j: (j, 0, 0))
    return pl.pallas_call(
        functools.partial(_qk_prep_kernel, dh=dh, slopes2=slopes2, bq=bq),
        grid=(T // br, 2),
        in_specs=[pl.BlockSpec((br, width), lambda i, j: (i, col_block0 + j)), spec, spec, spec],
        out_specs=pl.BlockSpec((br, 2 * width), lambda i, j: (i, j)),
        out_shape=jax.ShapeDtypeStruct((T, 4 * width), BF16),
        compiler_params=_params("parallel", "parallel"),
        name="qk_prep",
    )(u, gains.reshape(2, 1, dh).astype(F32), vec(base_q, base_k), vec(coef_q, coef_k))


def _diff_attn_bounded_kernel(lam_ref, slope_ref, q_ref, k_ref, v_ref, sub_ref, o_ref, acc_ref, l_ref, *, out_scale):
    bq = q_ref.shape[0]
    dk = q_ref.shape[1] // 2
    qi = pl.program_id(2)
    slope2 = slope_ref[0, pl.program_id(1)]
    visible = lax.broadcasted_iota(jnp.int32, (bq, bq), 1) <= lax.broadcasted_iota(jnp.int32, (bq, bq), 0)
    qs = [q_ref[:, c * dk:(c + 1) * dk] for c in range(2)]

    def step(blk, width, masked):
        cj = slope2 * ((blk // 2) * (2 * bq) - qi * bq).astype(F32)
        rows = pl.ds(pl.multiple_of(blk * bq, bq), width)
        vb = v_ref[rows, :]
        for c in range(2):
            s = lax.dot_general(qs[c], k_ref[rows, c * dk:(c + 1) * dk], (((1,), (1,)), ((), ())),
                                preferred_element_type=F32) + cj
            if masked:
                s = jnp.where(visible, s, -jnp.inf)
            p = jnp.exp2(s)
            l_ref[c] += sum(p[:, t:t + LANE] for t in range(0, width, LANE))
            acc_ref[c] += jnp.dot(p.astype(BF16), vb, preferred_element_type=F32)

    acc_ref[...] = jnp.zeros_like(acc_ref)
    l_ref[...] = jnp.zeros_like(l_ref)

    n_pairs = qi // 2

    def two_pairs(i, carry):
        step(4 * i, 2 * bq, masked=False)
        step(4 * i + 2, 2 * bq, masked=False)
        return carry

    lax.fori_loop(0, n_pairs // 2, two_pairs, 0)

    @pl.when(n_pairs % 2 == 1)
    def _():
        step(2 * (n_pairs - 1), 2 * bq, masked=False)

    @pl.when(qi % 2 == 1)
    def _():
        step(qi - 1, bq, masked=False)

    step(qi, bq, masked=True)
    l0 = jnp.sum(l_ref[0], axis=-1, keepdims=True)
    l1 = jnp.sum(l_ref[1], axis=-1, keepdims=True)
    o = acc_ref[0] / l0 - lam_ref[0, 0] * (acc_ref[1] / l1)
    o = o * lax.rsqrt(jnp.mean(o * o, axis=-1, keepdims=True) + EPS) * sub_ref[...] * out_scale
    o_ref[...] = o.astype(o_ref.dtype)


def _diff_attn_kernel(lam_ref, slope_ref, q_ref, k_ref, v_ref, sub_ref, o_ref, acc_ref, *, out_scale):
    bq = q_ref.shape[0]
    dk = q_ref.shape[1] // 2
    bk = bq
    qi = pl.program_id(2)
    slope2 = slope_ref[0, pl.program_id(1)]
    visible = lax.broadcasted_iota(jnp.int32, (bq, bk), 1) <= lax.broadcasted_iota(jnp.int32, (bq, bk), 0)
    qs = [q_ref[:, c * dk:(c + 1) * dk] for c in range(2)]

    def step(j, carry, masked):
        cj = slope2 * (((j // 2) * 2 - qi) * bk).astype(F32)
        rows = pl.ds(pl.multiple_of(j * bk, bk), bk)
        vb = v_ref[rows, :]
        out = []
        for c in range(2):
            m, l = carry[2 * c], carry[2 * c + 1]
            s = lax.dot_general(qs[c], k_ref[rows, c * dk:(c + 1) * dk], (((1,), (1,)), ((), ())),
                                preferred_element_type=F32)
            if masked:
                s = jnp.where(visible, s, -jnp.inf)
            m_new = jnp.maximum(m, jnp.max(s, axis=-1, keepdims=True) + cj)
            alpha = jnp.exp2(m - m_new)
            p = jnp.exp2(s - (m_new - cj))
            l = alpha * l + jnp.sum(p, axis=-1, keepdims=True)
            acc_ref[c] = alpha * acc_ref[c] + jnp.dot(p.astype(BF16), vb, preferred_element_type=F32)
            out += [m_new, l]
        return tuple(out)

    acc_ref[...] = jnp.zeros_like(acc_ref)
    neg = jnp.full((bq, 1), -jnp.inf, F32)
    zero = jnp.zeros((bq, 1), F32)
    carry = lax.fori_loop(0, qi, functools.partial(step, masked=False), (neg, zero, neg, zero))
    _, l0, _, l1 = step(qi, carry, masked=True)
    o = acc_ref[0] / l0 - lam_ref[0, 0] * (acc_ref[1] / l1)
    o = o * lax.rsqrt(jnp.mean(o * o, axis=-1, keepdims=True) + EPS) * sub_ref[...] * out_scale
    o_ref[...] = o.astype(o_ref.dtype)


def _alibi_slopes2(n_heads):
    return tuple(2.0 ** (-8.0 * (h + 1) / n_heads) * math.log2(math.e) for h in range(n_heads))


def diff_attention(qk, u, v_col_block0, lam, subln, lambda_init, batch, seq, n_heads, bq, bounded):
    T = qk.shape[0]
    dv = subln.shape[-1]
    nq = seq // bq
    slopes2 = jnp.asarray(_alibi_slopes2(n_heads), F32).reshape(1, n_heads)
    body = _diff_attn_bounded_kernel if bounded else _diff_attn_kernel
    scratch = [pltpu.VMEM((2, bq, dv), F32)] + ([pltpu.VMEM((2, bq, LANE), F32)] if bounded else [])
    return pl.pallas_call(
        functools.partial(body, out_scale=1.0 - lambda_init),
        grid=(batch, n_heads, nq),
        in_specs=[
            pl.BlockSpec(memory_space=pltpu.SMEM),
            pl.BlockSpec(memory_space=pltpu.SMEM),
            pl.BlockSpec((bq, 2 * dv), lambda b, h, i: (b * nq + i, h)),
            pl.BlockSpec((seq, 2 * dv), lambda b, h, i: (b, n_heads + h)),
            pl.BlockSpec((seq, dv), lambda b, h, i: (b, v_col_block0 + h)),
            pl.BlockSpec((1, dv), lambda b, h, i: (0, 0)),
        ],
        out_specs=pl.BlockSpec((bq, dv), lambda b, h, i: (b * nq + i, h)),
        out_shape=jax.ShapeDtypeStruct((T, n_heads * dv), BF16),
        scratch_shapes=scratch,
        compiler_params=_params("parallel", "parallel", "arbitrary"),
        name="diff_attention_bounded" if bounded else "diff_attention",
    )(lam.reshape(1, 1).astype(F32), slopes2, qk, qk, u, subln.reshape(1, dv).astype(F32))


def _proj_conv_kernel(x_ref, w_ref, cw_ref, cb_ref, o_ref, *buf_refs, tiles_per_seq):
    bm = x_ref.shape[0]
    kw = cw_ref.shape[0]
    chunk = buf_refs[0].shape[1]
    first = (pl.program_id(1) % tiles_per_seq) == 0

    @pl.when(first)
    def _():
        for buf_ref in buf_refs:
            buf_ref[0:8, :] = jnp.zeros((8, chunk), F32)

    @pl.when(jnp.logical_not(first))
    def _():
        for buf_ref in buf_refs:
            buf_ref[0:8, :] = buf_ref[bm:bm + 8, :]

    x = x_ref[...]
    off = 8 - (kw - 1)
    for c, buf_ref in enumerate(buf_refs):
        cols = slice(c * chunk, (c + 1) * chunk)
        buf_ref[8:8 + bm, :] = jnp.dot(x, w_ref[:, cols], preferred_element_type=F32)
        acc = cb_ref[:, cols] + cw_ref[0:1, cols] * buf_ref[off:off + bm, :]
        for k in range(1, kw):
            acc = acc + cw_ref[k:k + 1, cols] * buf_ref[off + k:off + k + bm, :]
        o_ref[:, cols] = _silu(acc).astype(o_ref.dtype)


def proj_conv_silu(xn, w, w_col0, n, conv_w, conv_b, seq):
    T, D = xn.shape
    bm = _tile(seq, FFN_BM, 8)
    bn = _tile(math.gcd(n, w_col0), PCONV_BN)
    chunk = _tile(bn, PCONV_CHUNK)
    jb = w_col0 // bn
    assert conv_w.shape[0] - 1 <= 8
    return pl.pallas_call(
        functools.partial(_proj_conv_kernel, tiles_per_seq=seq // bm),
        grid=(n // bn, T // bm),
        in_specs=[
            pl.BlockSpec((bm, D), lambda j, i: (i, 0)),
            pl.BlockSpec((D, bn), lambda j, i: (0, jb + j)),
            pl.BlockSpec((conv_w.shape[0], bn), lambda j, i: (0, j)),
            pl.BlockSpec((1, bn), lambda j, i: (0, j)),
        ],
        out_specs=pl.BlockSpec((bm, bn), lambda j, i: (i, j)),
        out_shape=jax.ShapeDtypeStruct((T, n), BF16),
        scratch_shapes=[pltpu.VMEM((bm + 8, chunk), F32)] * (bn // chunk),
        compiler_params=_params("parallel", "arbitrary"),
        name="proj_conv_silu",
    )(xn, w, conv_w.astype(F32), conv_b.reshape(1, n).astype(F32))


def _split3(x):
    hi = x.astype(BF16)
    r = x - hi.astype(F32)
    mid = r.astype(BF16)
    lo = (r - mid.astype(F32)).astype(BF16)
    return hi, mid, lo


def _softplus(x):
    return jnp.maximum(x, 0.0) + jnp.log1p(jnp.exp(-jnp.abs(x)))


def _ssd_kernel(x_ref, z_ref, b_ref, c_ref, dt_ref, dtt_ref, a_ref, at_ref, db_ref, dbt_ref, e_ref, dsk_ref, nw_ref,
                o_ref, state_ref, *, hpg, hd, L):
    hpv = LANE // hd

    @pl.when(pl.program_id(2) == 0)
    def _():
        state_ref[...] = jnp.zeros_like(state_ref)

    ri = lax.broadcasted_iota(jnp.int32, (L, L), 0)
    ci = lax.broadcasted_iota(jnp.int32, (L, L), 1)
    causal = ci <= ri
    tril = causal.astype(BF16)
    triu = (ri <= ci).astype(BF16)
    lane_head = lax.broadcasted_iota(jnp.int32, (L, LANE), 1) // hd

    def expand(v):
        return jnp.dot(jnp.concatenate(_split3(v), axis=1), e_ref[...], preferred_element_type=F32)

    for r0 in range(0, x_ref.shape[0], L):
        rows = slice(r0, r0 + L)
        dt = _softplus(dt_ref[0, 0, rows, :] + db_ref[0])
        dtt = _softplus(dtt_ref[0, 0, :, rows] + dbt_ref[0])
        a = dt * a_ref[0]
        at = dtt * at_ref[0]
        acum = sum(jnp.dot(tril, p, preferred_element_type=F32) for p in _split3(a))
        acumt = sum(jnp.dot(p, triu, preferred_element_type=F32) for p in _split3(at))
        a_last = acum[L - 1:L, :]
        ea_x = expand(jnp.exp(acum))
        wst_x = expand(jnp.exp(a_last - acum) * dt)

        bm = b_ref[rows, :]
        cm = c_ref[rows, :]
        x = x_ref[rows, :]
        x32 = x.astype(F32)
        state = state_ref[...]
        y = jnp.dot(cm, state.astype(BF16), preferred_element_type=F32) * ea_x + dsk_ref[...] * x32
        xw = (x32 * wst_x).astype(BF16)
        upd = lax.dot_general(bm, xw, (((0,), (0,)), ((), ())), preferred_element_type=F32)
        state_ref[...] = state * ea_x[L - 1:L, :] + upd

        cb = lax.dot_general(cm, bm, (((1,), (1,)), ((), ())), preferred_element_type=F32)
        cbm = jnp.where(causal, cb, 0.0)
        groups = []
        for g0 in range(0, hpg, hpv):
            xg = x[:, g0 * hd:(g0 + hpv) * hd]
            yg = None
            for r in range(hpv):
                j = g0 + r
                seg = jnp.minimum(acum[:, j:j + 1] - acumt[j:j + 1, :], 0.0)
                mj = (cbm * jnp.exp(seg) * dtt[j:j + 1, :]).astype(BF16)
                d = jnp.dot(mj, xg, preferred_element_type=F32)
                yg = d if yg is None else jnp.where(lane_head == r, d, yg)
            groups.append(yg)
        y = y + jnp.concatenate(groups, axis=1)

        g = y * _silu(z_ref[rows, :].astype(F32))
        g = g * lax.rsqrt(jnp.mean(g * g, axis=-1, keepdims=True) + EPS) * nw_ref[...]
        o_ref[rows, :] = g.astype(o_ref.dtype)


def ssd_scan(xbc, z, dt_raw, dt_bias, a_log, d_skip, norm_w, batch, seq, d_inner):
    T = xbc.shape[0]
    H = dt_raw.shape[-1]
    N = SSM_STATE
    G = (xbc.shape[1] - d_inner) // (2 * N)
    hpg = H // G
    hd = d_inner // H
    gw = hpg * hd
    L = _tile(seq, SSM_CHUNK, 8)
    lb = _tile(seq, SSM_BLOCK_CHUNKS * L, L)
    nc = seq // lb
    assert gw % LANE == 0 and gw % N == 0 and LANE % hd == 0
    b0 = d_inner // N
    expand = jnp.tile(jnp.repeat(jnp.eye(hpg, dtype=BF16), hd, axis=1), (3, 1))
    dtg = dt_raw.reshape(batch, seq, G, hpg).transpose(0, 2, 1, 3)
    dttg = dtg.transpose(0, 1, 3, 2)
    a = -jnp.exp(a_log.astype(F32)).reshape(G, 1, hpg)
    db = dt_bias.astype(F32).reshape(G, 1, hpg)
    dsk = jnp.repeat(d_skip.astype(F32), hd).reshape(1, d_inner)
    row = lambda b, g, c: (b * nc + c, g)
    return pl.pallas_call(
        functools.partial(_ssd_kernel, hpg=hpg, hd=hd, L=L),
        grid=(batch, G, nc),
        in_specs=[
            pl.BlockSpec((lb, gw), row),
            pl.BlockSpec((lb, gw), row),
            pl.BlockSpec((lb, N), lambda b, g, c: (b * nc + c, b0 + g)),
            pl.BlockSpec((lb, N), lambda b, g, c: (b * nc + c, b0 + G + g)),
            pl.BlockSpec((1, 1, lb, hpg), lambda b, g, c: (b, g, c, 0)),
            pl.BlockSpec((1, 1, hpg, lb), lambda b, g, c: (b, g, 0, c)),
            pl.BlockSpec((1, 1, hpg), lambda b, g, c: (g, 0, 0)),
            pl.BlockSpec((1, hpg, 1), lambda b, g, c: (g, 0, 0)),
            pl.BlockSpec((1, 1, hpg), lambda b, g, c: (g, 0, 0)),
            pl.BlockSpec((1, hpg, 1), lambda b, g, c: (g, 0, 0)),
            pl.BlockSpec((3 * hpg, gw), lambda b, g, c: (0, 0)),
            pl.BlockSpec((1, gw), lambda b, g, c: (0, g)),
            pl.BlockSpec((1, gw), lambda b, g, c: (0, g)),
        ],
        out_specs=pl.BlockSpec((lb, gw), row),
        out_shape=jax.ShapeDtypeStruct((T, d_inner), BF16),
        scratch_shapes=[pltpu.VMEM((N, gw), F32)],
        compiler_params=_params("parallel", "parallel", "arbitrary"),
        name="ssd_scan",
    )(xbc, z, xbc, xbc, dtg, dttg, a, a.reshape(G, hpg, 1), db, db.reshape(G, hpg, 1), expand, dsk,
      norm_w.reshape(1, d_inner).astype(F32))


def _cross_kernel(h_ref, gx_ref, wq_ref, qn_ref, k_ref, v_ref, wo_ref, gf_ref, ho_ref, xn_ref, *, dh):
    h = h_ref[...]
    hn = (h * lax.rsqrt(jnp.mean(h * h, axis=-1, keepdims=True) + EPS) * gx_ref[...]).astype(BF16)
    q = jnp.dot(hn, wq_ref[...], preferred_element_type=F32)
    xd = q.shape[-1]
    heads = []
    for c0 in range(0, xd, dh):
        qh = q[:, c0:c0 + dh]
        qh = qh * lax.rsqrt(jnp.mean(qh * qh, axis=-1, keepdims=True) + EPS) * qn_ref[...]
        s = lax.dot_general(qh.astype(BF16), k_ref[0, :, c0:c0 + dh], (((1,), (1,)), ((), ())),
                            preferred_element_type=F32)
        s = s - jnp.max(s, axis=-1, keepdims=True)
        p = jnp.exp(s)
        p = p / jnp.sum(p, axis=-1, keepdims=True)
        heads.append(jnp.dot(p.astype(BF16), v_ref[0, :, c0:c0 + dh], preferred_element_type=F32))
    o = jnp.concatenate(heads, axis=-1).astype(BF16)
    hnew = h + jnp.dot(o, wo_ref[...], preferred_element_type=F32)
    ho_ref[...] = hnew
    xn = hnew * lax.rsqrt(jnp.mean(hnew * hnew, axis=-1, keepdims=True) + EPS) * gf_ref[...]
    xn_ref[...] = xn.astype(xn_ref.dtype)


def cross_attention(h, g_cross, wq, q_norm, k, v, wo, g_ffn, seq):
    T, D = h.shape
    X = wq.shape[1]
    M = k.shape[1]
    dh = q_norm.shape[-1]
    bm = _tile(seq, CROSS_ROWS, 8)
    per_seq = seq // bm
    const = lambda i: (0, 0)
    resident = dict(pipeline_mode=pl.Buffered(1))
    return pl.pallas_call(
        functools.partial(_cross_kernel, dh=dh),
        grid=(T // bm,),
        in_specs=[
            pl.BlockSpec((bm, D), lambda i: (i, 0)),
            pl.BlockSpec((1, D), const),
            pl.BlockSpec((D, X), const, **resident),
            pl.BlockSpec((1, dh), const),
            pl.BlockSpec((1, M, X), lambda i: (i // per_seq, 0, 0)),
            pl.BlockSpec((1, M, X), lambda i: (i // per_seq, 0, 0)),
            pl.BlockSpec((X, D), const, **resident),
            pl.BlockSpec((1, D), const),
        ],
        out_specs=[pl.BlockSpec((bm, D), lambda i: (i, 0)), pl.BlockSpec((bm, D), lambda i: (i, 0))],
        out_shape=[jax.ShapeDtypeStruct((T, D), F32), jax.ShapeDtypeStruct((T, D), BF16)],
        compiler_params=_params("parallel"),
        name="cross_attention",
    )(h, g_cross.reshape(1, D).astype(F32), wq, q_norm.reshape(1, dh).astype(F32), k, v, wo,
      g_ffn.reshape(1, D).astype(F32))


def _ffn_in_kernel(x_ref, wa_ref, wb_ref, cw_ref, cb_ref, o_ref, buf_ref, *, tiles_per_seq):
    bm = x_ref.shape[0]
    kw = cw_ref.shape[0]
    first = (pl.program_id(1) % tiles_per_seq) == 0
    @pl.when(first)
    def _():
        buf_ref[0:8, :] = jnp.zeros((8, buf_ref.shape[1]), F32)

    @pl.when(jnp.logical_not(first))
    def _():
        buf_ref[0:8, :] = buf_ref[bm:bm + 8, :]

    x = x_ref[...]
    buf_ref[8:8 + bm, :] = jnp.dot(x, wa_ref[...], preferred_element_type=F32)
    b = jnp.dot(x, wb_ref[...], preferred_element_type=F32)
    off = 8 - (kw - 1)
    acc = cb_ref[...] + cw_ref[0:1, :] * buf_ref[off:off + bm, :]
    for k in range(1, kw):
        acc = acc + cw_ref[k:k + 1, :] * buf_ref[off + k:off + k + bm, :]
    o_ref[...] = (_silu(acc) * b).astype(o_ref.dtype)


def ffn_in(xn, w_ab, conv_w, conv_b, seq):
    T, D = xn.shape
    F = w_ab.shape[1] // 2
    bm = _tile(seq, FFN_BM, 8)
    bn = _tile(F, FFN_BN)
    nj = F // bn
    return pl.pallas_call(
        functools.partial(_ffn_in_kernel, tiles_per_seq=seq // bm),
        grid=(nj, T // bm),
        in_specs=[
            pl.BlockSpec((bm, D), lambda j, i: (i, 0)),
            pl.BlockSpec((D, bn), lambda j, i: (0, j)),
            pl.BlockSpec((D, bn), lambda j, i: (0, nj + j)),
            pl.BlockSpec((conv_w.shape[0], bn), lambda j, i: (0, j)),
            pl.BlockSpec((1, bn), lambda j, i: (0, j)),
        ],
        out_specs=pl.BlockSpec((bm, bn), lambda j, i: (i, j)),
        out_shape=jax.ShapeDtypeStruct((T, F), BF16),
        scratch_shapes=[pltpu.VMEM((bm + 8, bn), F32)],
        compiler_params=_params("parallel", "arbitrary"),
        name="ffn_in",
    )(xn, w_ab, w_ab, conv_w, conv_b)


def _pad_cols(w, n):
    return jnp.pad(w, ((0, 0), (0, n - w.shape[1])))


def kernel(x, mem, norm_mix, norm_cross, norm_mem, norm_ffn, xq_w, xk_w, xv_w, xo_w, xq_norm, xk_norm, ffn_in_w, ffn_conv_w, ffn_conv_b, ffn_out_w, ev_in_w, ev_out_w, cv_conv_w, cv_conv_b, cv_ln_g, cv_ln_b, da_q_norm, da_k_norm, da_lq1, da_lk1, da_lq2, da_lk2, da_subln, m_in_w, m_conv_w, m_conv_b, m_dt_bias, m_A_log, m_D, m_norm, m_out_w):
    batch, seq, D = x.shape
    depth = norm_mix.shape[0]
    T = batch * seq
    mem_len = mem.shape[1]
    x_dim = xq_w.shape[-1]
    x_dh = xq_norm.shape[-1]
    d_ff = ffn_conv_w.shape[-1]
    f_pad = -(-d_ff // FFN_PAD) * FFN_PAD
    conv_ch = cv_conv_w.shape[-1]
    da_dh = da_q_norm.shape[-1]
    da_dv = da_subln.shape[-1]
    da_qk = (ev_in_w.shape[-1] - 2 * conv_ch) // 3
    da_heads = da_qk // (2 * da_dh)
    d_inner = m_norm.shape[-1]
    ssm_heads = m_dt_bias.shape[-1]
    assert da_dv == 2 * da_dh and conv_ch % da_qk == 0 and da_qk % da_dv == 0

    h = x.reshape(T, D)
    mem2 = mem.reshape(batch * mem_len, D)

    for i in range(depth):
        xn = rmsnorm(h, norm_mix[i])
        if i % 2 == 0:
            e = i // 2
            lambda_init = 0.8 - 0.6 * math.exp(-0.3 * i)
            u = matmul(xn, cast_cols(ev_in_w, e))
            c = conformer_conv(u, cv_conv_w[e], cv_conv_b[e], cv_ln_g[e], cv_ln_b[e], seq)
            gains = jnp.stack([da_q_norm[e].astype(F32) * (da_dh ** -0.5 * math.log2(math.e)),
                               da_k_norm[e].astype(F32)])
            bq = _tile(seq, ATT_BQ, 8)
            assert seq % (2 * bq) == 0
            shift = 1.02 * da_dh * jnp.max(jnp.abs(gains[0])) * jnp.max(jnp.abs(gains[1])) + 0.1
            qk = qk_prep(u, gains, 2 * conv_ch // da_qk, da_qk, _alibi_slopes2(da_heads), bq, shift)
            lam = (jnp.exp(jnp.sum(da_lq1[e].astype(F32) * da_lk1[e].astype(F32)))
                   - jnp.exp(jnp.sum(da_lq2[e].astype(F32) * da_lk2[e].astype(F32))) + lambda_init)
            attn = functools.partial(diff_attention, qk, u, (2 * conv_ch + 2 * da_qk) // da_dv, lam, da_subln[e],
                                     lambda_init, batch, seq, da_heads, bq)
            o = lax.cond(shift <= ATT_MAX_SHIFT, lambda: attn(True), lambda: attn(False))
            h = matmul((c, o), cast_cols(ev_out_w, e), res=h, out_dtype=F32)
        else:
            o_ = i // 2
            w_in = cast_cols(m_in_w, o_)
            n_xbc = w_in.shape[1] - d_inner - ssm_heads
            z = matmul(xn, w_in, n=d_inner)
            xbc = proj_conv_silu(xn, w_in, d_inner, n_xbc, m_conv_w[o_], m_conv_b[o_], seq)
            dt_raw = matmul(xn, w_in, w_col0=d_inner + n_xbc, n=ssm_heads, out_dtype=F32)
            g = ssd_scan(xbc, z, dt_raw, m_dt_bias[o_], m_A_log[o_], m_D[o_], m_norm[o_], batch, seq, d_inner)
            h = matmul(g, cast_cols(m_out_w, o_), res=h, out_dtype=F32, bn=MM_WIDE_BN, bk=MM_WIDE_BK)

        memn = rmsnorm(mem2, norm_mem[i])
        kv = matmul(memn, jnp.concatenate([xk_w[i], xv_w[i]], axis=1).astype(BF16), out_dtype=F32)
        kx = headnorm(kv, xk_norm[i].reshape(1, x_dh), 0, x_dim).reshape(batch, mem_len, x_dim)
        vx = kv[:, x_dim:].astype(BF16).reshape(batch, mem_len, x_dim)
        h, xn = cross_attention(h, norm_cross[i], cast_cols(xq_w, i), xq_norm[i].astype(F32) * (x_dh ** -0.5),
                                kx, vx, cast_cols(xo_w, i), norm_ffn[i], seq)

        w_ab = cast_cols(ffn_in_w, i, out_cols=f_pad, halves=2)
        cw = _pad_cols(ffn_conv_w[i].astype(F32), f_pad)
        cb = _pad_cols(ffn_conv_b[i].astype(F32).reshape(1, d_ff), f_pad)
        act = ffn_in(xn, w_ab, cw, cb, seq)
        w_out = cast_rows(ffn_out_w, i, f_pad)
        h = matmul(act, w_out, res=h, out_dtype=F32, bn=MM_WIDE_BN, bk=MM_WIDE_BK)
    return h.reshape(batch, seq, D)
```
